```python
import jax, jax.numpy as jnp
from jax import lax
import numpy as np

D_MODEL = 1024
BATCH = 4
SEQ = 4096
DEPTH = 1

D_RNN = D_MODEL
N_RNN_BLOCKS = 4
RNN_BLOCK = D_RNN // N_RNN_BLOCKS
CONV_WIDTH = 4
C_RG = 8.0
MIN_RAD, MAX_RAD = 0.9, 0.999
D_POOL = D_MODEL
POOL_WINDOWS = (2, 4, 8, 16)
N_POOL_GROUPS = len(POOL_WINDOWS)
POOL_GROUP = D_POOL // N_POOL_GROUPS
N_BRANCHES = 2
D_FF = 4 * D_MODEL
N_MOD = 6
D_IN = 2 * D_RNN + D_POOL + N_BRANCHES * D_MODEL
EPS = 1e-6

kernel_name = "hybrid_rglru_pool_gated_block"


def rmsnorm(x, g):
    xf = x.astype(jnp.float32)
    y = xf * lax.rsqrt(jnp.mean(xf * xf, axis=-1, keepdims=True) + EPS)
    return (y * g.astype(jnp.float32)).astype(x.dtype)


def block_diag(x, w):
    b, s, _ = x.shape
    g, dg, _ = w.shape
    y = jnp.einsum('bsgi,gij->bsgj', x.reshape(b, s, g, dg), w)
    return y.reshape(b, s, g * dg)


def causal_conv(x, w, bias):
    k = w.shape[0]
    s = x.shape[1]
    xp = jnp.pad(x, ((0, 0), (k - 1, 0), (0, 0)))
    y = bias
    for j in range(k):
        y = y + xp[:, j:j + s] * w[j]
    return y


def rg_lru(x, w_a, b_a, w_x, b_x, a_param):
    xf = x.astype(jnp.float32)
    r = jax.nn.sigmoid(block_diag(xf, w_a.astype(jnp.float32)) + b_a.astype(jnp.float32))
    i = jax.nn.sigmoid(block_diag(xf, w_x.astype(jnp.float32)) + b_x.astype(jnp.float32))
    log_a = -C_RG * r * jax.nn.softplus(a_param.astype(jnp.float32))
    a = jnp.exp(log_a)
    mult = jnp.sqrt(-jnp.expm1(2.0 * log_a))
    first = (jnp.arange(x.shape[1]) == 0)[None, :, None]
    mult = jnp.where(first, 1.0, mult)
    bval = xf * i * mult

    def combine(l, rr):
        a1, b1 = l
        a2, b2 = rr
        return a1 * a2, a2 * b1 + b2

    _, h = lax.associative_scan(combine, (a, bval), axis=1)
    return h.astype(x.dtype)


def multiscale_pool(u, w_pool, b_pool, pool_scale):
    uf = u.astype(jnp.float32)
    s = u.shape[1]
    pos = jnp.arange(s, dtype=jnp.float32)[None, :, None]
    outs = []
    for gi, win in enumerate(POOL_WINDOWS):
        seg = uf[..., gi * POOL_GROUP:(gi + 1) * POOL_GROUP]
        cs = jnp.cumsum(seg, axis=1)
        cs_prev = jnp.pad(cs, ((0, 0), (win, 0), (0, 0)))[:, :s]
        cnt = jnp.minimum(pos + 1.0, float(win))
        outs.append((cs - cs_prev) / cnt - seg)
    p = jnp.concatenate(outs, axis=-1)
    p = block_diag(p, w_pool.astype(jnp.float32)) + b_pool.astype(jnp.float32)
    return (p * pool_scale.astype(jnp.float32)).astype(u.dtype)


def setup_inputs(seed: int = 0) -> dict:
    key = jax.random.key(seed)
    ks = jax.random.split(key, 24)
    f32 = jnp.float32
    L = DEPTH

    def nrm(k, shape, fan_in):
        return jax.random.normal(k, shape, f32) * (fan_in ** -0.5)

    u = jax.random.uniform(ks[10], (L, D_RNN), f32)
    a_real = 0.5 * jnp.log(u * (MAX_RAD ** 2 - MIN_RAD ** 2) + MIN_RAD ** 2)
    a_param = jnp.log(jnp.expm1(-a_real))
    return {
        "x": jax.random.normal(ks[0], (BATCH, SEQ, D_MODEL), f32),
        "c": jax.random.normal(ks[1], (BATCH, D_MODEL), f32),
        "norm_mix_g": 1.0 + 0.05 * jax.random.normal(ks[2], (L, D_MODEL), f32),
        "norm_mlp_g": 1.0 + 0.05 * jax.random.normal(ks[3], (L, D_MODEL), f32),
        "w_ada": nrm(ks[4], (L, D_MODEL, N_MOD * D_MODEL), D_MODEL),
        "b_ada": 0.02 * jax.random.normal(ks[5], (L, N_MOD * D_MODEL), f32),
        "w_in": nrm(ks[6], (L, D_MODEL, D_IN), D_MODEL),
        "conv_w": nrm(ks[7], (L, CONV_WIDTH, D_RNN), CONV_WIDTH),
        "conv_b": 0.02 * jax.random.normal(ks[8], (L, D_RNN), f32),
        "w_rg_a": nrm(ks[9], (L, N_RNN_BLOCKS, RNN_BLOCK, RNN_BLOCK), RNN_BLOCK),
        "b_rg_a": 0.02 * jax.random.normal(ks[11], (L, D_RNN), f32),
        "w_rg_x": nrm(ks[12], (L, N_RNN_BLOCKS, RNN_BLOCK, RNN_BLOCK), RNN_BLOCK),
        "b_rg_x": 0.02 * jax.random.normal(ks[13], (L, D_RNN), f32),
        "a_param": a_param,
        "w_branch_a": nrm(ks[14], (L, D_RNN, D_MODEL), D_RNN),
        "w_pool": nrm(ks[15], (L, N_POOL_GROUPS, POOL_GROUP, POOL_GROUP), POOL_GROUP),
        "b_pool": 0.02 * jax.random.normal(ks[16], (L, D_POOL), f32),
        "pool_scale": 1.0 + 0.1 * jax.random.normal(ks[17], (L, D_POOL), f32),
        "w_branch_b": nrm(ks[18], (L, D_POOL, D_MODEL), D_POOL),
        "w_out": nrm(ks[19], (L, D_MODEL, D_MODEL), D_MODEL),
        "w_up": nrm(ks[20], (L, D_MODEL, D_FF), D_MODEL),
        "w_down": nrm(ks[21], (L, D_FF, D_MODEL), D_FF),
        "final_g": 1.0 + 0.05 * jax.random.normal(ks[22], (D_MODEL,), f32),
    }


def reference(x, c, norm_mix_g, norm_mlp_g, w_ada, b_ada, w_in, conv_w, conv_b,
              w_rg_a, b_rg_a, w_rg_x, b_rg_x, a_param, w_branch_a, w_pool, b_pool,
              pool_scale, w_branch_b, w_out, w_up, w_down, final_g):
    c_act = jax.nn.silu(c)
    for l in range(DEPTH):
        mod = c_act @ w_ada[l] + b_ada[l]
        sh1, sc1, gt1, sh2, sc2, gt2 = [m[:, None, :] for m in jnp.split(mod, N_MOD, axis=-1)]

        h = rmsnorm(x, norm_mix_g[l]) * (1.0 + sc1) + sh1
        proj = h @ w_in[l]
        x_rnn, y_rnn, u_pool, g_a, g_b = jnp.split(
            proj, np.cumsum([D_RNN, D_RNN, D_POOL, D_MODEL]).tolist(), axis=-1)

        xr = causal_conv(x_rnn, conv_w[l], conv_b[l])
        hr = rg_lru(xr, w_rg_a[l], b_rg_a[l], w_rg_x[l], b_rg_x[l], a_param[l])
        branch_a = (jax.nn.gelu(y_rnn) * hr) @ w_branch_a[l]

        pooled = multiscale_pool(u_pool, w_pool[l], b_pool[l], pool_scale[l])
        branch_b = pooled @ w_branch_b[l]

        merged = jax.nn.sigmoid(g_a) * branch_a + jax.nn.sigmoid(g_b) * branch_b
        x = x + gt1 * (merged @ w_out[l])

        h = rmsnorm(x, norm_mlp_g[l]) * (1.0 + sc2) + sh2
        ff = jnp.square(jax.nn.relu(h @ w_up[l]))
        x = x + gt2 * (ff @ w_down[l])
    return rmsnorm(x, final_g)
```

```python
import functools

import jax
import jax.numpy as jnp
from jax import lax
from jax.experimental import pallas as pl
from jax.experimental.pallas import tpu as pltpu

D_MODEL = 1024
N_MOD = 6
N_BLOCKS = 4
BLOCK = D_MODEL // N_BLOCKS
CONV_WIDTH = 4
C_RG = 8.0
POOL_WINDOWS = (2, 4, 8, 16)
D_FF = 4 * D_MODEL
EPS = 1e-6

SUBLANES = 8
CONV_HALO = SUBLANES
POOL_HALO = 2 * SUBLANES
T_MIX = 256
T_MLP = 512
FF_CHUNK = 1024
VMEM_LIMIT = 56 * 1024 * 1024

BF16 = jnp.bfloat16
F32 = jnp.float32


def _dot(a, b):
    return jnp.dot(a, b, preferred_element_type=F32)


def _rms_normalize(x):
    return x * lax.rsqrt(jnp.mean(x * x, axis=-1, keepdims=True) + EPS)


def _delay_rows(v, k):
    if k % SUBLANES == 0:
        return jnp.concatenate([v[:k], v[:v.shape[0] - k]], axis=0)
    return pltpu.roll(v, k, axis=0)


def _ada_kernel(c_ref, w_ref, b_ref, o_ref):
    c = c_ref[...]
    c_act = c * jax.nn.sigmoid(c)
    o_ref[...] = _dot(c_act.astype(BF16), w_ref[...].astype(BF16)) + b_ref[...]


def _ada(c, w_ada, b_ada):
    batch = c.shape[0]
    n_out = w_ada.shape[1]
    return pl.pallas_call(
        _ada_kernel,
        grid=(n_out // D_MODEL,),
        in_specs=[
            pl.BlockSpec((batch, D_MODEL), lambda n: (0, 0)),
            pl.BlockSpec((D_MODEL, D_MODEL), lambda n: (0, n)),
            pl.BlockSpec((1, D_MODEL), lambda n: (0, n)),
        ],
        out_specs=pl.BlockSpec((batch, D_MODEL), lambda n: (0, n)),
        out_shape=jax.ShapeDtypeStruct((batch, n_out), F32),
        compiler_params=pltpu.CompilerParams(dimension_semantics=("arbitrary",)),
        name="ada_mod",
    )(c, w_ada, b_ada)


def _linear_scan(a, b, h0):
    t = a.shape[0]
    rows = lax.broadcasted_iota(jnp.int32, a.shape, 0)
    d = 1
    while d < t:
        if d % SUBLANES == 0:
            a_prev = jnp.concatenate([jnp.ones((d, a.shape[1]), F32), a[:t - d]], axis=0)
            b_prev = jnp.concatenate([jnp.zeros((d, a.shape[1]), F32), b[:t - d]], axis=0)
            b = a * b_prev + b
            a = a * a_prev
        else:
            keep = rows < d
            a_prev = pltpu.roll(a, d, axis=0)
            b_prev = pltpu.roll(b, d, axis=0)
            b = jnp.where(keep, b, a * b_prev + b)
            a = jnp.where(keep, a, a * a_prev)
        d *= 2
    return a * h0 + b


def _mixer_kernel(x_ref, mod_ref, g_ref, w_in_ref, conv_w_ref, conv_b_ref, w_rg_ref,
                  b_rga_ref, b_rgx_ref, a_param_ref, w_ba_ref, w_pool_ref, b_pool_ref,
                  pscale_ref, w_bb_ref, w_out_ref, o_ref, xtail_ref, utail_ref, h_ref):
    j = pl.program_id(1)
    t = x_ref.shape[1]
    d = D_MODEL

    @pl.when(j == 0)
    def _():
        xtail_ref[...] = jnp.zeros_like(xtail_ref)
        utail_ref[...] = jnp.zeros_like(utail_ref)
        h_ref[...] = jnp.zeros_like(h_ref)

    x = x_ref[0]
    sh1 = mod_ref[0, 0:1, :]
    sc1 = mod_ref[0, 1:2, :]
    gt1 = mod_ref[0, 2:3, :]
    h = _rms_normalize(x) * g_ref[...] * (1.0 + sc1) + sh1
    hb = h.astype(BF16)

    x_rnn = _dot(hb, w_in_ref[:, 0:d])
    ext = jnp.concatenate([xtail_ref[...], x_rnn], axis=0)
    xtail_ref[...] = x_rnn[t - CONV_HALO:]
    xr = conv_b_ref[...] + x_rnn * conv_w_ref[CONV_WIDTH - 1:CONV_WIDTH, :]
    for k in range(1, CONV_WIDTH):
        xr = xr + _delay_rows(ext, k)[CONV_HALO:] * conv_w_ref[CONV_WIDTH - 1 - k:CONV_WIDTH - k, :]

    xrb = xr.astype(BF16)
    gates = [_dot(xrb[:, k * BLOCK:(k + 1) * BLOCK], w_rg_ref[k]) for k in range(N_BLOCKS)]
    r_pre = jnp.concatenate([g[:, :BLOCK] for g in gates], axis=1)
    i_pre = jnp.concatenate([g[:, BLOCK:] for g in gates], axis=1)
    r = jax.nn.sigmoid(r_pre + b_rga_ref[...])
    i = jax.nn.sigmoid(i_pre + b_rgx_ref[...])
    log_a = r * (-C_RG * jax.nn.softplus(a_param_ref[...]))
    a = jnp.exp(log_a)
    mult = jnp.sqrt(1.0 - a * a)
    gated_x = xr * i
    bval = gated_x * mult
    is_start = (lax.broadcasted_iota(jnp.int32, (SUBLANES, d), 0) == 0) & (j == 0)
    bval = jnp.concatenate(
        [jnp.where(is_start, gated_x[:SUBLANES], bval[:SUBLANES]), bval[SUBLANES:]], axis=0)
    hr = _linear_scan(a, bval, h_ref[SUBLANES - 1:SUBLANES, :])
    h_ref[...] = hr[t - SUBLANES:]
    y_rnn = _dot(hb, w_in_ref[:, d:2 * d])
    branch_a = _dot((jax.nn.gelu(y_rnn) * hr).astype(BF16), w_ba_ref[...])

    u = _dot(hb, w_in_ref[:, 2 * d:3 * d])
    uext = jnp.concatenate([utail_ref[...], u], axis=0)
    utail_ref[...] = u[t - POOL_HALO:]
    pos = (lax.broadcasted_iota(jnp.int32, (t, BLOCK), 0) + j * t + 1).astype(F32)
    pooled = []
    acc = uext
    win = 1
    for gi, target in enumerate(POOL_WINDOWS):
        while win < target:
            acc = acc + _delay_rows(acc, win)
            win *= 2
        seg = acc[POOL_HALO:, :BLOCK]
        acc = acc[:, BLOCK:]
        cnt = jnp.minimum(pos, float(target))
        p = seg / cnt - u[:, gi * BLOCK:(gi + 1) * BLOCK]
        pooled.append(_dot(p.astype(BF16), w_pool_ref[gi]))
    pooled = (jnp.concatenate(pooled, axis=1) + b_pool_ref[...]) * pscale_ref[...]
    branch_b = _dot(pooled.astype(BF16), w_bb_ref[...])

    g_a = _dot(hb, w_in_ref[:, 3 * d:4 * d])
    g_b = _dot(hb, w_in_ref[:, 4 * d:5 * d])
    merged = jax.nn.sigmoid(g_a) * branch_a + jax.nn.sigmoid(g_b) * branch_b
    o_ref[0] = x + gt1 * _dot(merged.astype(BF16), w_out_ref[...])


def _const_spec(shape):
    zeros = (0,) * len(shape)
    return pl.BlockSpec(shape, lambda b, j: zeros, pipeline_mode=pl.Buffered(1))


def _mixer(x, mod, g, w_in, conv_w, conv_b, w_rg, b_rga, b_rgx, a_param, w_ba, w_pool,
           b_pool, pscale, w_bb, w_out):
    batch, seq, d = x.shape
    weights = (g, w_in, conv_w, conv_b, w_rg, b_rga, b_rgx, a_param, w_ba, w_pool, b_pool,
               pscale, w_bb, w_out)
    return pl.pallas_call(
        _mixer_kernel,
        grid=(batch, seq // T_MIX),
        in_specs=[
            pl.BlockSpec((1, T_MIX, d), lambda b, j: (b, j, 0)),
            pl.BlockSpec((1, N_MOD, d), lambda b, j: (b, 0, 0)),
        ] + [_const_spec(w.shape) for w in weights],
        out_specs=pl.BlockSpec((1, T_MIX, d), lambda b, j: (b, j, 0)),
        out_shape=jax.ShapeDtypeStruct(x.shape, F32),
        scratch_shapes=[
            pltpu.VMEM((CONV_HALO, d), F32),
            pltpu.VMEM((POOL_HALO, d), F32),
            pltpu.VMEM((SUBLANES, d), F32),
        ],
        compiler_params=pltpu.CompilerParams(
            dimension_semantics=("arbitrary", "arbitrary"), vmem_limit_bytes=VMEM_LIMIT),
        name="mixer",
    )(x, mod, *weights)


def _mlp_kernel(x_ref, mod_ref, g_ref, w_up_ref, w_down_ref, gf_ref, o_ref):
    x = x_ref[0]
    sh2 = mod_ref[0, 3:4, :]
    sc2 = mod_ref[0, 4:5, :]
    gt2 = mod_ref[0, 5:6, :]
    h = _rms_normalize(x) * g_ref[...] * (1.0 + sc2) + sh2
    hb = h.astype(BF16)
    acc = jnp.zeros(x.shape, F32)
    for n in range(D_FF // FF_CHUNK):
        up = _dot(hb, w_up_ref[:, n * FF_CHUNK:(n + 1) * FF_CHUNK])
        ff = jnp.square(jnp.maximum(up, 0.0)).astype(BF16)
        acc = acc + _dot(ff, w_down_ref[n * FF_CHUNK:(n + 1) * FF_CHUNK, :])
    x2 = x + gt2 * acc
    o_ref[0] = _rms_normalize(x2) * gf_ref[...]


def _mlp(x, mod, g, w_up, w_down, final_g):
    batch, seq, d = x.shape
    weights = (g, w_up, w_down, final_g)
    return pl.pallas_call(
        _mlp_kernel,
        grid=(batch, seq // T_MLP),
        in_specs=[
            pl.BlockSpec((1, T_MLP, d), lambda b, j: (b, j, 0)),
            pl.BlockSpec((1, N_MOD, d), lambda b, j: (b, 0, 0)),
        ] + [_const_spec(w.shape) for w in weights],
        out_specs=pl.BlockSpec((1, T_MLP, d), lambda b, j: (b, j, 0)),
        out_shape=jax.ShapeDtypeStruct(x.shape, F32),
        compiler_params=pltpu.CompilerParams(
            dimension_semantics=("arbitrary", "arbitrary"), vmem_limit_bytes=VMEM_LIMIT),
        name="mlp_final",
    )(x, mod, *weights)


def kernel(x, c, norm_mix_g, norm_mlp_g, w_ada, b_ada, w_in, conv_w, conv_b, w_rg_a, b_rg_a,
           w_rg_x, b_rg_x, a_param, w_branch_a, w_pool, b_pool, pool_scale, w_branch_b, w_out,
           w_up, w_down, final_g):
    assert w_in.shape[0] == 1, "only DEPTH == 1 is supported"
    batch = x.shape[0]
    mod = _ada(c, w_ada[0], b_ada).reshape(batch, N_MOD, D_MODEL)
    w_rg = jnp.concatenate([w_rg_a[0], w_rg_x[0]], axis=-1).astype(BF16)
    x = _mixer(
        x, mod, norm_mix_g, w_in[0].astype(BF16), conv_w[0], conv_b, w_rg, b_rg_a, b_rg_x,
        a_param, w_branch_a[0].astype(BF16), w_pool[0].astype(BF16), b_pool, pool_scale,
        w_branch_b[0].astype(BF16), w_out[0].astype(BF16))
    return _mlp(x, mod, norm_mlp_g, w_up[0].astype(BF16), w_down[0].astype(BF16),
                final_g[None, :])
```

```python
import jax
import jax.numpy as jnp
from jax import lax
from jax.experimental import pallas as pl
from jax.experimental.pallas import tpu as pltpu

D_MODEL = 1024
N_MOD = 6
N_BLOCKS = 4
BLOCK = D_MODEL // N_BLOCKS
CONV_WIDTH = 4
C_RG = 8.0
POOL_WINDOWS = (2, 4, 8, 16)
D_FF = 4 * D_MODEL
EPS = 1e-6

SUBLANES = 8
LANES = 128
N_SLABS = D_MODEL // LANES
SLABS_PER_BLOCK = BLOCK // LANES
CONV_HIST = SUBLANES
POOL_PAD = SUBLANES
POOL_HIST = 2 * SUBLANES
POOL_BASE = POOL_PAD + POOL_HIST
N_CHUNKS = SUBLANES
CHUNK_PITCH_PAD = 4
T_MIX = 256
T_MLP = 512
FF_CHUNK = 1024
VMEM_LIMIT = 56 * 1024 * 1024

BF16 = jnp.bfloat16
F32 = jnp.float32


def _dot(a, b):
    return jnp.dot(a, b, preferred_element_type=F32)


def _rms_normalize(x):
    return x * lax.rsqrt(jnp.mean(x * x, axis=-1, keepdims=True) + EPS)


def _slab_rows(ref, c, start, size):
    return ref[pl.ds(c, 1, stride=2), pl.ds(start, size), :][0]


def _store_slab_rows(ref, c, start, val):
    ref[pl.ds(c, 1, stride=2), pl.ds(start, val.shape[0]), :] = val[None]


def _load_rows(ref, start, size):
    return jnp.concatenate(
        [_slab_rows(ref, c, start, size) for c in range(ref.shape[0])], axis=1)


def _store_rows(ref, start, val):
    for c in range(ref.shape[0]):
        ref[c, start:start + val.shape[0], :] = val[:, c * LANES:(c + 1) * LANES]


def _ada_kernel(c_ref, w_ref, b_ref, o_ref):
    c = c_ref[...]
    c_act = c * jax.nn.sigmoid(c)
    o_ref[...] = _dot(c_act.astype(BF16), w_ref[...].astype(BF16)) + b_ref[...]


def _ada(c, w_ada, b_ada):
    batch = c.shape[0]
    n_out = w_ada.shape[1]
    return pl.pallas_call(
        _ada_kernel,
        grid=(n_out // D_MODEL,),
        in_specs=[
            pl.BlockSpec((batch, D_MODEL), lambda n: (0, 0)),
            pl.BlockSpec((D_MODEL, D_MODEL), lambda n: (0, n)),
            pl.BlockSpec((1, D_MODEL), lambda n: (0, n)),
        ],
        out_specs=pl.BlockSpec((batch, D_MODEL), lambda n: (0, n)),
        out_shape=jax.ShapeDtypeStruct((batch, n_out), F32),
        compiler_params=pltpu.CompilerParams(dimension_semantics=("arbitrary",)),
        name="ada_mod",
    )(c, w_ada, b_ada)


def _linear_scan(a, b, carry_ref, a_s, b_s, h_s, p_s):
    t = a.shape[0]
    ch = t // N_CHUNKS
    pitch = ch + CHUNK_PITCH_PAD
    for c in range(N_SLABS):
        for r in range(N_CHUNKS):
            rows, lanes = slice(r * ch, (r + 1) * ch), slice(c * LANES, (c + 1) * LANES)
            _store_slab_rows(a_s, c, r * pitch, a[rows, lanes])
            _store_slab_rows(b_s, c, r * pitch, b[rows, lanes])

    def step(g):
        return pl.ds(g, N_CHUNKS, stride=pitch)

    h = b_s[:, step(0), :]
    p = a_s[:, step(0), :]
    h_s[:, step(0), :] = h
    p_s[:, step(0), :] = p
    for g in range(1, ch):
        a_g = a_s[:, step(g), :]
        h = a_g * h + b_s[:, step(g), :]
        p = a_g * p
        h_s[:, step(g), :] = h
        p_s[:, step(g), :] = p

    carry = carry_ref[:, 0:1, :]
    carries = []
    for r in range(N_CHUNKS):
        carries.append(carry)
        carry = p[:, r:r + 1, :] * carry + h[:, r:r + 1, :]
    carry_ref[:, 0:1, :] = carry

    cols = []
    for c in range(N_SLABS):
        pieces = [_slab_rows(h_s, c, r * pitch, ch) + _slab_rows(p_s, c, r * pitch, ch) * carries[r][c]
                  for r in range(N_CHUNKS)]
        cols.append(jnp.concatenate(pieces, axis=0))
    return jnp.concatenate(cols, axis=1)


def _mixer_kernel(x_ref, mod_ref, g_ref, w_in_ref, conv_w_ref, conv_b_ref, w_rg_ref,
                  b_rga_ref, b_rgx_ref, a_param_ref, w_ba_ref, w_pool_ref, b_pool_ref,
                  pscale_ref, w_bb_ref, w_out_ref, o_ref,
                  x_s, u_s, s2_s, s4_s, carry_ref, a_s, b_s, h_s, p_s):
    j = pl.program_id(1)
    t = x_ref.shape[1]
    d = D_MODEL

    @pl.when(j == 0)
    def _():
        x_s[:, 0:CONV_HIST, :] = jnp.zeros((N_SLABS, CONV_HIST, LANES), F32)
        u_s[:, 0:POOL_BASE, :] = jnp.zeros((u_s.shape[0], POOL_BASE, LANES), F32)
        s2_s[:, 0:POOL_PAD, :] = jnp.zeros((s2_s.shape[0], POOL_PAD, LANES), F32)
        s4_s[:, 0:POOL_PAD, :] = jnp.zeros((s4_s.shape[0], POOL_PAD, LANES), F32)
        carry_ref[...] = jnp.zeros_like(carry_ref)

    x = x_ref[0]
    sh1 = mod_ref[0, 0:1, :]
    sc1 = mod_ref[0, 1:2, :]
    gt1 = mod_ref[0, 2:3, :]
    h = _rms_normalize(x) * (g_ref[...] * (1.0 + sc1)) + sh1
    hb = h.astype(BF16)

    x_rnn = _dot(hb, w_in_ref[:, 0:d])
    _store_rows(x_s, CONV_HIST, x_rnn)
    xr = conv_b_ref[...] + x_rnn * conv_w_ref[CONV_WIDTH - 1:CONV_WIDTH, :]
    for k in range(1, CONV_WIDTH):
        xr = xr + _load_rows(x_s, CONV_HIST - k, t) * conv_w_ref[CONV_WIDTH - 1 - k:CONV_WIDTH - k, :]
    _store_rows(x_s, 0, x_rnn[t - CONV_HIST:])

    xrb = xr.astype(BF16)
    gates = [_dot(xrb[:, k * BLOCK:(k + 1) * BLOCK], w_rg_ref[k]) for k in range(N_BLOCKS)]
    r_pre = jnp.concatenate([g[:, :BLOCK] for g in gates], axis=1)
    i_pre = jnp.concatenate([g[:, BLOCK:] for g in gates], axis=1)
    r = jax.nn.sigmoid(r_pre + b_rga_ref[...])
    i = jax.nn.sigmoid(i_pre + b_rgx_ref[...])
    log_a = r * (-C_RG * jax.nn.softplus(a_param_ref[...]))
    a = jnp.exp(log_a)
    z = 1.0 - a * a
    mult = jnp.where(z > 0.0, z * lax.rsqrt(z), 0.0)
    gated_x = xr * i
    bval = gated_x * mult
    is_start = (lax.broadcasted_iota(jnp.int32, (SUBLANES, d), 0) == 0) & (j == 0)
    bval = jnp.concatenate(
        [jnp.where(is_start, gated_x[:SUBLANES], bval[:SUBLANES]), bval[SUBLANES:]], axis=0)
    hr = _linear_scan(a, bval, carry_ref, a_s, b_s, h_s, p_s)
    y_rnn = _dot(hb, w_in_ref[:, d:2 * d])
    branch_a = _dot((jax.nn.gelu(y_rnn) * hr).astype(BF16), w_ba_ref[...])

    u = _dot(hb, w_in_ref[:, 2 * d:3 * d])
    _store_rows(u_s, POOL_BASE, u)
    ext = POOL_HIST + t
    s2 = _load_rows(u_s, POOL_PAD, ext) + _load_rows(u_s, POOL_PAD - 1, ext)
    _store_rows(s2_s, POOL_PAD, s2[:, BLOCK:])
    s4 = s2[:, BLOCK:] + _load_rows(s2_s, POOL_PAD - 2, ext)
    _store_rows(s4_s, POOL_PAD, s4[:, BLOCK:])
    s8 = s4[:, BLOCK:] + _load_rows(s4_s, POOL_PAD - 4, ext)
    s16 = s8[POOL_HIST:, BLOCK:] + s8[POOL_HIST - SUBLANES:ext - SUBLANES, BLOCK:]
    _store_rows(u_s, POOL_PAD, u[t - POOL_HIST:])
    sums = (s2[POOL_HIST:, :BLOCK], s4[POOL_HIST:, :BLOCK], s8[POOL_HIST:, :BLOCK], s16)
    head_pos = (lax.broadcasted_iota(jnp.int32, (POOL_HIST, BLOCK), 0) + j * t + 1).astype(F32)
    pooled = []
    for gi, win in enumerate(POOL_WINDOWS):
        head = sums[gi][:POOL_HIST] / jnp.minimum(head_pos, float(win))
        mean = jnp.concatenate([head, sums[gi][POOL_HIST:] * (1.0 / win)], axis=0)
        p = mean - u[:, gi * BLOCK:(gi + 1) * BLOCK]
        pooled.append(_dot(p.astype(BF16), w_pool_ref[gi]))
    pooled = (jnp.concatenate(pooled, axis=1) + b_pool_ref[...]) * pscale_ref[...]
    branch_b = _dot(pooled.astype(BF16), w_bb_ref[...])

    g_a = _dot(hb, w_in_ref[:, 3 * d:4 * d])
    g_b = _dot(hb, w_in_ref[:, 4 * d:5 * d])
    merged = jax.nn.sigmoid(g_a) * branch_a + jax.nn.sigmoid(g_b) * branch_b
    o_ref[0] = x + gt1 * _dot(merged.astype(BF16), w_out_ref[...])


def _const_spec(shape):
    zeros = (0,) * len(shape)
    return pl.BlockSpec(shape, lambda b, j: zeros, pipeline_mode=pl.Buffered(1))


def _mixer(x, mod, g, w_in, conv_w, conv_b, w_rg, b_rga, b_rgx, a_param, w_ba, w_pool,
           b_pool, pscale, w_bb, w_out):
    batch, seq, d = x.shape
    weights = (g, w_in, conv_w, conv_b, w_rg, b_rga, b_rgx, a_param, w_ba, w_pool, b_pool,
               pscale, w_bb, w_out)
    scan_rows = N_CHUNKS * (T_MIX // N_CHUNKS + CHUNK_PITCH_PAD)
    scan_scratch = pltpu.VMEM((N_SLABS, scan_rows, LANES), F32)
    return pl.pallas_call(
        _mixer_kernel,
        grid=(batch, seq // T_MIX),
        in_specs=[
            pl.BlockSpec((1, T_MIX, d), lambda b, j: (b, j, 0)),
            pl.BlockSpec((1, N_MOD, d), lambda b, j: (b, 0, 0)),
        ] + [_const_spec(w.shape) for w in weights],
        out_specs=pl.BlockSpec((1, T_MIX, d), lambda b, j: (b, j, 0)),
        out_shape=jax.ShapeDtypeStruct(x.shape, F32),
        scratch_shapes=[
            pltpu.VMEM((N_SLABS, CONV_HIST + T_MIX, LANES), F32),
            pltpu.VMEM((N_SLABS, POOL_BASE + T_MIX, LANES), F32),
            pltpu.VMEM((N_SLABS - SLABS_PER_BLOCK, POOL_BASE + T_MIX, LANES), F32),
            pltpu.VMEM((N_SLABS - 2 * SLABS_PER_BLOCK, POOL_BASE + T_MIX, LANES), F32),
            pltpu.VMEM((N_SLABS, SUBLANES, LANES), F32),
            scan_scratch, scan_scratch, scan_scratch, scan_scratch,
        ],
        compiler_params=pltpu.CompilerParams(
            dimension_semantics=("arbitrary", "arbitrary"), vmem_limit_bytes=VMEM_LIMIT),
        name="mixer",
    )(x, mod, *weights)


def _mlp_kernel(x_ref, mod_ref, g_ref, w_up_ref, w_down_ref, gf_ref, o_ref):
    x = x_ref[0]
    sh2 = mod_ref[0, 3:4, :]
    sc2 = mod_ref[0, 4:5, :]
    gt2 = mod_ref[0, 5:6, :]
    h = _rms_normalize(x) * (g_ref[...] * (1.0 + sc2)) + sh2
    hb = h.astype(BF16)
    acc = jnp.zeros(x.shape, F32)
    for n in range(D_FF // FF_CHUNK):
        up = _dot(hb, w_up_ref[:, n * FF_CHUNK:(n + 1) * FF_CHUNK])
        ff = jnp.square(jnp.maximum(up, 0.0)).astype(BF16)
        acc = acc + _dot(ff, w_down_ref[n * FF_CHUNK:(n + 1) * FF_CHUNK, :])
    x2 = x + gt2 * acc
    o_ref[0] = _rms_normalize(x2) * gf_ref[...]


def _mlp(x, mod, g, w_up, w_down, final_g):
    batch, seq, d = x.shape
    weights = (g, w_up, w_down, final_g)
    return pl.pallas_call(
        _mlp_kernel,
        grid=(batch, seq // T_MLP),
        in_specs=[
            pl.BlockSpec((1, T_MLP, d), lambda b, j: (b, j, 0)),
            pl.BlockSpec((1, N_MOD, d), lambda b, j: (b, 0, 0)),
        ] + [_const_spec(w.shape) for w in weights],
        out_specs=pl.BlockSpec((1, T_MLP, d), lambda b, j: (b, j, 0)),
        out_shape=jax.ShapeDtypeStruct(x.shape, F32),
        compiler_params=pltpu.CompilerParams(
            dimension_semantics=("arbitrary", "arbitrary"), vmem_limit_bytes=VMEM_LIMIT),
        name="mlp_final",
    )(x, mod, *weights)


def kernel(x, c, norm_mix_g, norm_mlp_g, w_ada, b_ada, w_in, conv_w, conv_b, w_rg_a, b_rg_a,
           w_rg_x, b_rg_x, a_param, w_branch_a, w_pool, b_pool, pool_scale, w_branch_b, w_out,
           w_up, w_down, final_g):
    assert w_in.shape[0] == 1, "only DEPTH == 1 is supported"
    batch = x.shape[0]
    mod = _ada(c, w_ada[0], b_ada).reshape(batch, N_MOD, D_MODEL)
    w_rg = jnp.concatenate([w_rg_a[0], w_rg_x[0]], axis=-1).astype(BF16)
    x = _mixer(
        x, mod, norm_mix_g, w_in[0].astype(BF16), conv_w[0], conv_b, w_rg, b_rg_a, b_rg_x,
        a_param, w_branch_a[0].astype(BF16), w_pool[0].astype(BF16), b_pool, pool_scale,
        w_branch_b[0].astype(BF16), w_out[0].astype(BF16))
    return _mlp(x, mod, norm_mlp_g, w_up[0].astype(BF16), w_down[0].astype(BF16),
                final_g[None, :])
```

```python
import functools

import jax
import jax.numpy as jnp
from jax import lax
from jax.experimental import pallas as pl
from jax.experimental.pallas import tpu as pltpu

D_MODEL = 1024
N_MOD = 6
N_BLOCKS = 4
BLOCK = D_MODEL // N_BLOCKS
CONV_WIDTH = 4
C_RG = 8.0
POOL_WINDOWS = (2, 4, 8, 16)
D_FF = 4 * D_MODEL
EPS = 1e-6

SUBLANES = 8
LANES = 128
N_SLABS = D_MODEL // LANES
SLABS_PER_BLOCK = BLOCK // LANES
HALF = D_MODEL // 2
BLOCKS_PER_HALF = HALF // BLOCK
SLABS_PER_HALF = HALF // LANES
CONV_HIST = SUBLANES
POOL_PAD = SUBLANES
POOL_HIST = 2 * SUBLANES
POOL_BASE = POOL_PAD + POOL_HIST
N_CHUNKS = SUBLANES
CHUNK_PITCH_PAD = 4
T_BLOCK = 256
FF_CHUNK = HALF
VMEM_LIMIT = 60 * 1024 * 1024

BF16 = jnp.bfloat16
F32 = jnp.float32


def _dot(a, b):
    return jnp.dot(a, b, preferred_element_type=F32)


def _rms_normalize(x):
    return x * lax.rsqrt(jnp.mean(x * x, axis=-1, keepdims=True) + EPS)


def _sigmoid(x):
    return 0.5 * jnp.tanh(0.5 * x) + 0.5


def _cols(k, width):
    return slice(k * width, (k + 1) * width)


def _slab_rows(ref, c, start, size):
    return ref[pl.ds(c, 1, stride=2), pl.ds(start, size), :][0]


def _store_slab_rows(ref, c, start, val):
    ref[pl.ds(c, 1, stride=2), pl.ds(start, val.shape[0]), :] = val[None]


def _load_rows(ref, first_slab, n_slabs, start, size):
    return jnp.concatenate(
        [_slab_rows(ref, first_slab + c, start, size) for c in range(n_slabs)], axis=1)


def _store_rows(ref, first_slab, start, val):
    for c in range(val.shape[1] // LANES):
        ref[first_slab + c, start:start + val.shape[0], :] = val[:, _cols(c, LANES)]


def _ada_kernel(c_ref, w_ref, b_ref, o_ref):
    c = c_ref[...]
    c_act = c * jax.nn.sigmoid(c)
    o_ref[...] = _dot(c_act.astype(BF16), w_ref[...].astype(BF16)) + b_ref[...]


def _ada(c, w_ada, b_ada):
    batch = c.shape[0]
    n_out = w_ada.shape[1]
    return pl.pallas_call(
        _ada_kernel,
        grid=(n_out // D_MODEL,),
        in_specs=[
            pl.BlockSpec((batch, D_MODEL), lambda n: (0, 0)),
            pl.BlockSpec((D_MODEL, D_MODEL), lambda n: (0, n)),
            pl.BlockSpec((1, D_MODEL), lambda n: (0, n)),
        ],
        out_specs=pl.BlockSpec((batch, D_MODEL), lambda n: (0, n)),
        out_shape=jax.ShapeDtypeStruct((batch, n_out), F32),
        compiler_params=pltpu.CompilerParams(dimension_semantics=("arbitrary",)),
        name="ada_mod",
    )(c, w_ada, b_ada)


def _linear_scan(a, b, carry_ref, first_slab, a_s, b_s):
    t = a.shape[0]
    n_slabs = a.shape[1] // LANES
    slabs = slice(first_slab, first_slab + n_slabs)
    ch = t // N_CHUNKS
    pitch = ch + CHUNK_PITCH_PAD
    for c in range(n_slabs):
        for r in range(N_CHUNKS):
            rows = slice(r * ch, (r + 1) * ch)
            _store_slab_rows(a_s, first_slab + c, r * pitch, a[rows, _cols(c, LANES)])
            _store_slab_rows(b_s, first_slab + c, r * pitch, b[rows, _cols(c, LANES)])

    def step(g):
        return pl.ds(g, N_CHUNKS, stride=pitch)

    h = b_s[slabs, step(0), :]
    p = a_s[slabs, step(0), :]
    for g in range(1, ch):
        a_g = a_s[slabs, step(g), :]
        h = a_g * h + b_s[slabs, step(g), :]
        p = a_g * p
        b_s[slabs, step(g), :] = h
        a_s[slabs, step(g), :] = p

    carry = carry_ref[slabs, 0:1, :]
    carries = []
    for r in range(N_CHUNKS):
        carries.append(carry)
        carry = p[:, r:r + 1, :] * carry + h[:, r:r + 1, :]
    carry_ref[slabs, 0:1, :] = carry

    cols = []
    for c in range(n_slabs):
        pieces = [_slab_rows(b_s, first_slab + c, r * pitch, ch)
                  + _slab_rows(a_s, first_slab + c, r * pitch, ch) * carries[r][c]
                  for r in range(N_CHUNKS)]
        cols.append(jnp.concatenate(pieces, axis=0))
    return jnp.concatenate(cols, axis=1)


def _merge_streams(streams):
    work = {name: [mxu + MXU_PER_VEC * vec for mxu, vec in costs]
            for name, (_, costs) in streams.items()}
    done = {name: 0.0 for name in streams}
    stage = {name: 0 for name in streams}
    for gen, _ in streams.values():
        next(gen)
    while stage:
        def progress_after_half(name):
            return (done[name] + work[name][stage[name]] / 2) / sum(work[name])
        name = min(stage, key=progress_after_half)
        done[name] += work[name][stage[name]]
        stage[name] += 1
        finished = next(streams[name][0], _DONE) is _DONE
        assert finished == (stage[name] == len(work[name])), f"stage costs of {name} are stale"
        if finished:
            del stage[name]


_DONE = object()
MXU_PER_VEC = 2.2
_HEAD_COSTS = ((0, 400),) + ((512, 30), (0, 300), (128, 0), (0, 600), (0, 350)) * 2
_TAIL_COSTS = (((512, 0), (0, 250), (512, 60)) * 2 + ((512, 0), (64, 250), (512, 60)) * 2
               + ((512, 150), (512, 200), (512, 60)) * 2 + ((0, 150),))
_MLP_COSTS = ((0, 400),) + ((512, 100), (512, 60)) * (D_FF // FF_CHUNK) + ((0, 500),)


def _layer_kernel(blocks_per_seq, x_ref, mod_h_ref, mod_t_ref, mod_m_ref,
                  g1_ref, w_in_ref, conv_w_ref, conv_b_ref, w_rg_ref, b_rga_ref, b_rgx_ref,
                  a_param_ref, w_ba_ref, w_pool_ref, b_pool_ref, pscale_ref, w_bb_ref, w_out_ref,
                  g2_ref, w_up_ref, w_down_ref, gf_ref, o_ref,
                  x_s, u_s, s2_s, s4_s, carry_ref, a_s, b_s, x_q, hb_q, hr_q, x1_q):
    s = pl.program_id(0)
    j_head = s % blocks_per_seq
    j_tail = (s + blocks_per_seq - 1) % blocks_per_seq
    t = x_ref.shape[1]
    d = D_MODEL

    @pl.when(j_head == 0)
    def _():
        x_s[:, 0:CONV_HIST, :] = jnp.zeros((N_SLABS, CONV_HIST, LANES), F32)
        carry_ref[...] = jnp.zeros_like(carry_ref)

    @pl.when((j_tail == 0) | (s == 0))
    def _():
        u_s[:, 0:POOL_BASE, :] = jnp.zeros((u_s.shape[0], POOL_BASE, LANES), F32)
        s2_s[:, 0:POOL_PAD, :] = jnp.zeros((s2_s.shape[0], POOL_PAD, LANES), F32)
        s4_s[:, 0:POOL_PAD, :] = jnp.zeros((s4_s.shape[0], POOL_PAD, LANES), F32)

    @pl.when(s == 0)
    def _():
        x_q[...] = jnp.zeros_like(x_q)
        hb_q[...] = jnp.zeros_like(hb_q)
        hr_q[...] = jnp.zeros_like(hr_q)
        x1_q[...] = jnp.zeros_like(x1_q)

    x_t = x_q[...]
    hb_t = hb_q[...]
    hr_t = hr_q[...]
    x1_m = x1_q[...]

    def head():
        x = x_ref[0]
        yield
        x_q[...] = x
        sh1 = mod_h_ref[0, 0:1, :]
        sc1 = mod_h_ref[0, 1:2, :]
        h = _rms_normalize(x) * (g1_ref[...] * (1.0 + sc1)) + sh1
        hb = h.astype(BF16)
        hb_q[...] = hb
        softplus_a = jax.nn.softplus(a_param_ref[...])
        for g in range(D_MODEL // HALF):
            cols = _cols(g, HALF)
            first_slab = g * SLABS_PER_HALF
            yield
            x_rnn = _dot(hb, w_in_ref[:, cols])
            yield
            _store_rows(x_s, first_slab, CONV_HIST, x_rnn)
            xr = conv_b_ref[:, cols] + x_rnn * conv_w_ref[CONV_WIDTH - 1:CONV_WIDTH, cols]
            for k in range(1, CONV_WIDTH):
                w_k = conv_w_ref[CONV_WIDTH - 1 - k:CONV_WIDTH - k, cols]
                xr = xr + _load_rows(x_s, first_slab, SLABS_PER_HALF, CONV_HIST - k, t) * w_k
            _store_rows(x_s, first_slab, 0, x_rnn[t - CONV_HIST:])
            xrb = xr.astype(BF16)
            yield
            gates = [_dot(xrb[:, _cols(k, BLOCK)], w_rg_ref[g * BLOCKS_PER_HALF + k])
                     for k in range(BLOCKS_PER_HALF)]
            r_pre = jnp.concatenate([gk[:, :BLOCK] for gk in gates], axis=1)
            i_pre = jnp.concatenate([gk[:, BLOCK:] for gk in gates], axis=1)
            yield
            r = _sigmoid(r_pre + b_rga_ref[:, cols])
            i = _sigmoid(i_pre + b_rgx_ref[:, cols])
            log_a = r * (-C_RG * softplus_a[:, cols])
            a = jnp.exp(log_a)
            z = 1.0 - a * a
            mult = jnp.where(z > 0.0, z * lax.rsqrt(z), 0.0)
            gated_x = xr * i
            bval = gated_x * mult
            is_start = (lax.broadcasted_iota(jnp.int32, (SUBLANES, HALF), 0) == 0) & (j_head == 0)
            bval = jnp.concatenate(
                [jnp.where(is_start, gated_x[:SUBLANES], bval[:SUBLANES]), bval[SUBLANES:]],
                axis=0)
            yield
            hr_q[:, cols] = _linear_scan(a, bval, carry_ref, first_slab, a_s, b_s)

    def pool_half(g, u):
        ext = POOL_HIST + t
        first_slab = g * SLABS_PER_HALF
        _store_rows(u_s, first_slab, POOL_BASE, u)
        s2 = (_load_rows(u_s, first_slab, SLABS_PER_HALF, POOL_PAD, ext)
              + _load_rows(u_s, first_slab, SLABS_PER_HALF, POOL_PAD - 1, ext))
        _store_rows(u_s, first_slab, POOL_PAD, u[t - POOL_HIST:])
        if g == 0:
            _store_rows(s2_s, 0, POOL_PAD, s2[:, BLOCK:])
            s4 = s2[:, BLOCK:] + _load_rows(s2_s, 0, SLABS_PER_BLOCK, POOL_PAD - 2, ext)
            sums = (s2[POOL_HIST:, :BLOCK], s4[POOL_HIST:])
        else:
            _store_rows(s2_s, SLABS_PER_BLOCK, POOL_PAD, s2)
            s4 = s2 + _load_rows(s2_s, SLABS_PER_BLOCK, SLABS_PER_HALF, POOL_PAD - 2, ext)
            _store_rows(s4_s, 0, POOL_PAD, s4)
            s8 = s4 + _load_rows(s4_s, 0, SLABS_PER_HALF, POOL_PAD - 4, ext)
            s16 = s8[POOL_HIST:, BLOCK:] + s8[POOL_HIST - SUBLANES:ext - SUBLANES, BLOCK:]
            sums = (s8[POOL_HIST:, :BLOCK], s16)
        head_pos = lax.broadcasted_iota(jnp.int32, (POOL_HIST, BLOCK), 0) + (j_tail * t + 1)
        head_pos = head_pos.astype(F32)
        pooled = []
        for k in range(BLOCKS_PER_HALF):
            gi = g * BLOCKS_PER_HALF + k
            win = POOL_WINDOWS[gi]
            head_mean = sums[k][:POOL_HIST] / jnp.minimum(head_pos, float(win))
            mean = jnp.concatenate([head_mean, sums[k][POOL_HIST:] * (1.0 / win)], axis=0)
            p = mean - u[:, _cols(k, BLOCK)]
            pooled.append(_dot(p.astype(BF16), w_pool_ref[gi]))
        cols = _cols(g, HALF)
        return (jnp.concatenate(pooled, axis=1) + b_pool_ref[:, cols]) * pscale_ref[:, cols]

    def tail():
        n_halves = D_MODEL // HALF
        branch_a = branch_b = out = None
        for g in range(n_halves):
            cols = _cols(g, HALF)
            yield
            y = _dot(hb_t, w_in_ref[:, d + g * HALF:d + (g + 1) * HALF])
            yield
            gated = (jax.nn.gelu(y) * hr_t[:, cols]).astype(BF16)
            yield
            part = _dot(gated, w_ba_ref[cols, :])
            branch_a = part if branch_a is None else branch_a + part
        for g in range(n_halves):
            cols = _cols(g, HALF)
            yield
            u = _dot(hb_t, w_in_ref[:, 2 * d + g * HALF:2 * d + (g + 1) * HALF])
            yield
            pooled = pool_half(g, u).astype(BF16)
            yield
            part = _dot(pooled, w_bb_ref[cols, :])
            branch_b = part if branch_b is None else branch_b + part
        for g in range(n_halves):
            cols = _cols(g, HALF)
            yield
            g_a = _dot(hb_t, w_in_ref[:, 3 * d + g * HALF:3 * d + (g + 1) * HALF])
            merged = _sigmoid(g_a) * branch_a[:, cols]
            yield
            g_b = _dot(hb_t, w_in_ref[:, 4 * d + g * HALF:4 * d + (g + 1) * HALF])
            merged = (merged + _sigmoid(g_b) * branch_b[:, cols]).astype(BF16)
            yield
            part = _dot(merged, w_out_ref[cols, :])
            out = part if out is None else out + part
        yield
        x1_q[...] = x_t + mod_t_ref[0, 2:3, :] * out

    def mlp():
        yield
        sh2 = mod_m_ref[0, 3:4, :]
        sc2 = mod_m_ref[0, 4:5, :]
        h = _rms_normalize(x1_m) * (g2_ref[...] * (1.0 + sc2)) + sh2
        hb = h.astype(BF16)
        acc = None
        for n in range(D_FF // FF_CHUNK):
            yield
            up = _dot(hb, w_up_ref[:, _cols(n, FF_CHUNK)])
            ff = jnp.square(jnp.maximum(up, 0.0)).astype(BF16)
            yield
            part = _dot(ff, w_down_ref[_cols(n, FF_CHUNK), :])
            acc = part if acc is None else acc + part
        yield
        x2 = x1_m + mod_m_ref[0, 5:6, :] * acc
        o_ref[0] = _rms_normalize(x2) * gf_ref[...]

    _merge_streams({"head": (head(), _HEAD_COSTS), "tail": (tail(), _TAIL_COSTS),
                    "mlp": (mlp(), _MLP_COSTS)})


def _const_spec(shape):
    zeros = (0,) * len(shape)
    return pl.BlockSpec(shape, lambda *_: zeros, pipeline_mode=pl.Buffered(1))


def _layer(x, mod, mixer_weights, mlp_weights):
    batch, seq, d = x.shape
    assert seq % T_BLOCK == 0
    blocks_per_seq = seq // T_BLOCK
    n_blocks = batch * blocks_per_seq
    weights = tuple(mixer_weights) + tuple(mlp_weights)

    def clamp(block):
        return jnp.clip(block, 0, n_blocks - 1)

    def x_map(lag):
        def index_map(s):
            block = clamp(s - lag)
            return block // blocks_per_seq, block % blocks_per_seq, 0
        return index_map

    def mod_map(lag):
        return lambda s: (clamp(s - lag) // blocks_per_seq, 0, 0)

    scan_rows = N_CHUNKS * (T_BLOCK // N_CHUNKS + CHUNK_PITCH_PAD)
    scan_scratch = pltpu.VMEM((N_SLABS, scan_rows, LANES), F32)
    return pl.pallas_call(
        functools.partial(_layer_kernel, blocks_per_seq),
        grid=(n_blocks + 2,),
        in_specs=[
            pl.BlockSpec((1, T_BLOCK, d), x_map(0)),
            pl.BlockSpec((1, N_MOD, d), mod_map(0)),
            pl.BlockSpec((1, N_MOD, d), mod_map(1)),
            pl.BlockSpec((1, N_MOD, d), mod_map(2)),
        ] + [_const_spec(w.shape) for w in weights],
        out_specs=pl.BlockSpec((1, T_BLOCK, d), x_map(2)),
        out_shape=jax.ShapeDtypeStruct(x.shape, F32),
        scratch_shapes=[
            pltpu.VMEM((N_SLABS, CONV_HIST + T_BLOCK, LANES), F32),
            pltpu.VMEM((N_SLABS, POOL_BASE + T_BLOCK, LANES), F32),
            pltpu.VMEM((N_SLABS - SLABS_PER_BLOCK, POOL_BASE + T_BLOCK, LANES), F32),
            pltpu.VMEM((SLABS_PER_HALF, POOL_BASE + T_BLOCK, LANES), F32),
            pltpu.VMEM((N_SLABS, SUBLANES, LANES), F32),
            scan_scratch, scan_scratch,
            pltpu.VMEM((T_BLOCK, d), F32),
            pltpu.VMEM((T_BLOCK, d), BF16),
            pltpu.VMEM((T_BLOCK, d), F32),
            pltpu.VMEM((T_BLOCK, d), F32),
        ],
        compiler_params=pltpu.CompilerParams(
            dimension_semantics=("arbitrary",), vmem_limit_bytes=VMEM_LIMIT),
        name="layer",
    )(x, mod, mod, mod, *weights)


def kernel(x, c, norm_mix_g, norm_mlp_g, w_ada, b_ada, w_in, conv_w, conv_b, w_rg_a, b_rg_a,
           w_rg_x, b_rg_x, a_param, w_branch_a, w_pool, b_pool, pool_scale, w_branch_b, w_out,
           w_up, w_down, final_g):
    assert w_in.shape[0] == 1, "only DEPTH == 1 is supported"
    batch = x.shape[0]
    mod = _ada(c, w_ada[0], b_ada).reshape(batch, N_MOD, D_MODEL)
    w_rg = jnp.concatenate([w_rg_a[0], w_rg_x[0]], axis=-1).astype(BF16)
    mixer_weights = (
        norm_mix_g, w_in[0].astype(BF16), conv_w[0], conv_b, w_rg, b_rg_a, b_rg_x, a_param,
        w_branch_a[0].astype(BF16), w_pool[0].astype(BF16), b_pool, pool_scale,
        w_branch_b[0].astype(BF16), w_out[0].astype(BF16))
    mlp_weights = (norm_mlp_g, w_up[0].astype(BF16), w_down[0].astype(BF16), final_g[None, :])
    return _layer(x, mod, mixer_weights, mlp_weights)
```

```python
import jax
import jax.numpy as jnp
from jax import lax
from jax.experimental import pallas as pl
from jax.experimental.pallas import tpu as pltpu

D_MODEL = 1024
N_MOD = 6
N_BLOCKS = 4
BLOCK = D_MODEL // N_BLOCKS
CONV_WIDTH = 4
C_RG = 8.0
POOL_WINDOWS = (2, 4, 8, 16)
D_FF = 4 * D_MODEL
EPS = 1e-6

SUBLANES = 8
LANES = 128
N_SLABS = D_MODEL // LANES
SLABS_PER_BLOCK = BLOCK // LANES
CONV_HIST = SUBLANES
POOL_PAD = SUBLANES
POOL_HIST = 2 * SUBLANES
POOL_BASE = POOL_PAD + POOL_HIST
N_CHUNKS = SUBLANES
CHUNK_PITCH_PAD = 4
T_MIX = 512
T_MLP = 1024
FF_CHUNK = 1024
VMEM_LIMIT = 56 * 1024 * 1024

BF16 = jnp.bfloat16
F32 = jnp.float32


def _dot(a, b):
    return jnp.dot(a, b, preferred_element_type=F32)


def _rms_normalize(x):
    return x * lax.rsqrt(jnp.mean(x * x, axis=-1, keepdims=True) + EPS)


def _slab_rows(ref, c, start, size):
    return ref[pl.ds(c, 1, stride=2), pl.ds(start, size), :][0]


def _store_slab_rows(ref, c, start, val):
    ref[pl.ds(c, 1, stride=2), pl.ds(start, val.shape[0]), :] = val[None]


def _load_rows(ref, start, size):
    return jnp.concatenate(
        [_slab_rows(ref, c, start, size) for c in range(ref.shape[0])], axis=1)


def _store_rows(ref, start, val):
    for c in range(ref.shape[0]):
        ref[c, start:start + val.shape[0], :] = val[:, c * LANES:(c + 1) * LANES]


def _ada_kernel(c_ref, w_ref, b_ref, o_ref):
    c = c_ref[...]
    c_act = c * jax.nn.sigmoid(c)
    o_ref[...] = _dot(c_act.astype(BF16), w_ref[...].astype(BF16)) + b_ref[...]


def _ada(c, w_ada, b_ada):
    batch = c.shape[0]
    n_out = w_ada.shape[1]
    return pl.pallas_call(
        _ada_kernel,
        grid=(n_out // D_MODEL,),
        in_specs=[
            pl.BlockSpec((batch, D_MODEL), lambda n: (0, 0)),
            pl.BlockSpec((D_MODEL, D_MODEL), lambda n: (0, n)),
            pl.BlockSpec((1, D_MODEL), lambda n: (0, n)),
        ],
        out_specs=pl.BlockSpec((batch, D_MODEL), lambda n: (0, n)),
        out_shape=jax.ShapeDtypeStruct((batch, n_out), F32),
        compiler_params=pltpu.CompilerParams(dimension_semantics=("arbitrary",)),
        name="ada_mod",
    )(c, w_ada, b_ada)


def _linear_scan(a, b, carry_ref, a_s, b_s, h_s, p_s):
    t = a.shape[0]
    ch = t // N_CHUNKS
    pitch = ch + CHUNK_PITCH_PAD
    for c in range(N_SLABS):
        for r in range(N_CHUNKS):
            rows, lanes = slice(r * ch, (r + 1) * ch), slice(c * LANES, (c + 1) * LANES)
            _store_slab_rows(a_s, c, r * pitch, a[rows, lanes])
            _store_slab_rows(b_s, c, r * pitch, b[rows, lanes])

    def step(g):
        return pl.ds(g, N_CHUNKS, stride=pitch)

    h = b_s[:, step(0), :]
    p = a_s[:, step(0), :]
    h_s[:, step(0), :] = h
    p_s[:, step(0), :] = p
    for g in range(1, ch):
        a_g = a_s[:, step(g), :]
        h = a_g * h + b_s[:, step(g), :]
        p = a_g * p
        h_s[:, step(g), :] = h
        p_s[:, step(g), :] = p

    carry = carry_ref[:, 0:1, :]
    carries = []
    for r in range(N_CHUNKS):
        carries.append(carry)
        carry = p[:, r:r + 1, :] * carry + h[:, r:r + 1, :]
    carry_ref[:, 0:1, :] = carry

    cols = []
    for c in range(N_SLABS):
        pieces = [_slab_rows(h_s, c, r * pitch, ch) + _slab_rows(p_s, c, r * pitch, ch) * carries[r][c]
                  for r in range(N_CHUNKS)]
        cols.append(jnp.concatenate(pieces, axis=0))
    return jnp.concatenate(cols, axis=1)


def _mixer_kernel(x_ref, mod_ref, g_ref, w_in_ref, conv_w_ref, conv_b_ref, w_rg_ref,
                  b_rga_ref, b_rgx_ref, a_param_ref, w_ba_ref, w_pool_ref, b_pool_ref,
                  pscale_ref, w_bb_ref, w_out_ref, o_ref,
                  x_s, u_s, s2_s, s4_s, carry_ref, a_s, b_s, h_s, p_s):
    j = pl.program_id(1)
    t = x_ref.shape[1]
    d = D_MODEL

    @pl.when(j == 0)
    def _():
        x_s[:, 0:CONV_HIST, :] = jnp.zeros((N_SLABS, CONV_HIST, LANES), F32)
        u_s[:, 0:POOL_BASE, :] = jnp.zeros((u_s.shape[0], POOL_BASE, LANES), F32)
        s2_s[:, 0:POOL_PAD, :] = jnp.zeros((s2_s.shape[0], POOL_PAD, LANES), F32)
        s4_s[:, 0:POOL_PAD, :] = jnp.zeros((s4_s.shape[0], POOL_PAD, LANES), F32)
        carry_ref[...] = jnp.zeros_like(carry_ref)

    x = x_ref[0]
    sh1 = mod_ref[0, 0:1, :]
    sc1 = mod_ref[0, 1:2, :]
    gt1 = mod_ref[0, 2:3, :]
    h = _rms_normalize(x) * (g_ref[...] * (1.0 + sc1)) + sh1
    hb = h.astype(BF16)

    x_rnn = _dot(hb, w_in_ref[:, 0:d])
    _store_rows(x_s, CONV_HIST, x_rnn)
    xr = conv_b_ref[...] + x_rnn * conv_w_ref[CONV_WIDTH - 1:CONV_WIDTH, :]
    for k in range(1, CONV_WIDTH):
        xr = xr + _load_rows(x_s, CONV_HIST - k, t) * conv_w_ref[CONV_WIDTH - 1 - k:CONV_WIDTH - k, :]
    _store_rows(x_s, 0, x_rnn[t - CONV_HIST:])

    xrb = xr.astype(BF16)
    gates = [_dot(xrb[:, k * BLOCK:(k + 1) * BLOCK], w_rg_ref[k]) for k in range(N_BLOCKS)]
    r_pre = jnp.concatenate([g[:, :BLOCK] for g in gates], axis=1)
    i_pre = jnp.concatenate([g[:, BLOCK:] for g in gates], axis=1)
    r = jax.nn.sigmoid(r_pre + b_rga_ref[...])
    i = jax.nn.sigmoid(i_pre + b_rgx_ref[...])
    log_a = r * (-C_RG * jax.nn.softplus(a_param_ref[...]))
    a = jnp.exp(log_a)
    z = 1.0 - a * a
    mult = jnp.where(z > 0.0, z * lax.rsqrt(z), 0.0)
    gated_x = xr * i
    bval = gated_x * mult
    is_start = (lax.broadcasted_iota(jnp.int32, (SUBLANES, d), 0) == 0) & (j == 0)
    bval = jnp.concatenate(
        [jnp.where(is_start, gated_x[:SUBLANES], bval[:SUBLANES]), bval[SUBLANES:]], axis=0)
    hr = _linear_scan(a, bval, carry_ref, a_s, b_s, h_s, p_s)
    y_rnn = _dot(hb, w_in_ref[:, d:2 * d])
    branch_a = _dot((jax.nn.gelu(y_rnn) * hr).astype(BF16), w_ba_ref[...])

    u = _dot(hb, w_in_ref[:, 2 * d:3 * d])
    _store_rows(u_s, POOL_BASE, u)
    ext = POOL_HIST + t
    s2 = _load_rows(u_s, POOL_PAD, ext) + _load_rows(u_s, POOL_PAD - 1, ext)
    _store_rows(s2_s, POOL_PAD, s2[:, BLOCK:])
    s4 = s2[:, BLOCK:] + _load_rows(s2_s, POOL_PAD - 2, ext)
    _store_rows(s4_s, POOL_PAD, s4[:, BLOCK:])
    s8 = s4[:, BLOCK:] + _load_rows(s4_s, POOL_PAD - 4, ext)
    s16 = s8[POOL_HIST:, BLOCK:] + s8[POOL_HIST - SUBLANES:ext - SUBLANES, BLOCK:]
    _store_rows(u_s, POOL_PAD, u[t - POOL_HIST:])
    sums = (s2[POOL_HIST:, :BLOCK], s4[POOL_HIST:, :BLOCK], s8[POOL_HIST:, :BLOCK], s16)
    head_pos = (lax.broadcasted_iota(jnp.int32, (POOL_HIST, BLOCK), 0) + j * t + 1).astype(F32)
    pooled = []
    for gi, win in enumerate(POOL_WINDOWS):
        head = sums[gi][:POOL_HIST] / jnp.minimum(head_pos, float(win))
        mean = jnp.concatenate([head, sums[gi][POOL_HIST:] * (1.0 / win)], axis=0)
        p = mean - u[:, gi * BLOCK:(gi + 1) * BLOCK]
        pooled.append(_dot(p.astype(BF16), w_pool_ref[gi]))
    pooled = (jnp.concatenate(pooled, axis=1) + b_pool_ref[...]) * pscale_ref[...]
    branch_b = _dot(pooled.astype(BF16), w_bb_ref[...])

    g_a = _dot(hb, w_in_ref[:, 3 * d:4 * d])
    g_b = _dot(hb, w_in_ref[:, 4 * d:5 * d])
    merged = jax.nn.sigmoid(g_a) * branch_a + jax.nn.sigmoid(g_b) * branch_b
    o_ref[0] = x + gt1 * _dot(merged.astype(BF16), w_out_ref[...])


def _const_spec(shape):
    zeros = (0,) * len(shape)
    return pl.BlockSpec(shape, lambda b, j: zeros, pipeline_mode=pl.Buffered(1))


def _mixer(x, mod, g, w_in, conv_w, conv_b, w_rg, b_rga, b_rgx, a_param, w_ba, w_pool,
           b_pool, pscale, w_bb, w_out):
    batch, seq, d = x.shape
    weights = (g, w_in, conv_w, conv_b, w_rg, b_rga, b_rgx, a_param, w_ba, w_pool, b_pool,
               pscale, w_bb, w_out)
    scan_rows = N_CHUNKS * (T_MIX // N_CHUNKS + CHUNK_PITCH_PAD)
    scan_scratch = pltpu.VMEM((N_SLABS, scan_rows, LANES), F32)
    return pl.pallas_call(
        _mixer_kernel,
        grid=(batch, seq // T_MIX),
        in_specs=[
            pl.BlockSpec((1, T_MIX, d), lambda b, j: (b, j, 0)),
            pl.BlockSpec((1, N_MOD, d), lambda b, j: (b, 0, 0)),
        ] + [_const_spec(w.shape) for w in weights],
        out_specs=pl.BlockSpec((1, T_MIX, d), lambda b, j: (b, j, 0)),
        out_shape=jax.ShapeDtypeStruct(x.shape, F32),
        scratch_shapes=[
            pltpu.VMEM((N_SLABS, CONV_HIST + T_MIX, LANES), F32),
            pltpu.VMEM((N_SLABS, POOL_BASE + T_MIX, LANES), F32),
            pltpu.VMEM((N_SLABS - SLABS_PER_BLOCK, POOL_BASE + T_MIX, LANES), F32),
            pltpu.VMEM((N_SLABS - 2 * SLABS_PER_BLOCK, POOL_BASE + T_MIX, LANES), F32),
            pltpu.VMEM((N_SLABS, SUBLANES, LANES), F32),
            scan_scratch, scan_scratch, scan_scratch, scan_scratch,
        ],
        compiler_params=pltpu.CompilerParams(
            dimension_semantics=("arbitrary", "arbitrary"), vmem_limit_bytes=VMEM_LIMIT),
        name="mixer",
    )(x, mod, *weights)


def _mlp_kernel(x_ref, mod_ref, g_ref, w_up_ref, w_down_ref, gf_ref, o_ref):
    x = x_ref[0]
    sh2 = mod_ref[0, 3:4, :]
    sc2 = mod_ref[0, 4:5, :]
    gt2 = mod_ref[0, 5:6, :]
    h = _rms_normalize(x) * (g_ref[...] * (1.0 + sc2)) + sh2
    hb = h.astype(BF16)
    acc = jnp.zeros(x.shape, F32)
    for n in range(D_FF // FF_CHUNK):
        up = _dot(hb, w_up_ref[:, n * FF_CHUNK:(n + 1) * FF_CHUNK])
        ff = jnp.square(jnp.maximum(up, 0.0)).astype(BF16)
        acc = acc + _dot(ff, w_down_ref[n * FF_CHUNK:(n + 1) * FF_CHUNK, :])
    x2 = x + gt2 * acc
    o_ref[0] = _rms_normalize(x2) * gf_ref[...]


def _mlp(x, mod, g, w_up, w_down, final_g):
    batch, seq, d = x.shape
    weights = (g, w_up, w_down, final_g)
    return pl.pallas_call(
        _mlp_kernel,
        grid=(batch, seq // T_MLP),
        in_specs=[
            pl.BlockSpec((1, T_MLP, d), lambda b, j: (b, j, 0)),
            pl.BlockSpec((1, N_MOD, d), lambda b, j: (b, 0, 0)),
        ] + [_const_spec(w.shape) for w in weights],
        out_specs=pl.BlockSpec((1, T_MLP, d), lambda b, j: (b, j, 0)),
        out_shape=jax.ShapeDtypeStruct(x.shape, F32),
        compiler_params=pltpu.CompilerParams(
            dimension_semantics=("arbitrary", "arbitrary"), vmem_limit_bytes=VMEM_LIMIT),
        name="mlp_final",
    )(x, mod, *weights)


def kernel(x, c, norm_mix_g, norm_mlp_g, w_ada, b_ada, w_in, conv_w, conv_b, w_rg_a, b_rg_a,
           w_rg_x, b_rg_x, a_param, w_branch_a, w_pool, b_pool, pool_scale, w_branch_b, w_out,
           w_up, w_down, final_g):
    assert w_in.shape[0] == 1, "only DEPTH == 1 is supported"
    batch = x.shape[0]
    mod = _ada(c, w_ada[0], b_ada).reshape(batch, N_MOD, D_MODEL)
    w_rg = jnp.concatenate([w_rg_a[0], w_rg_x[0]], axis=-1).astype(BF16)
    x = _mixer(
        x, mod, norm_mix_g, w_in[0].astype(BF16), conv_w[0], conv_b, w_rg, b_rg_a, b_rg_x,
        a_param, w_branch_a[0].astype(BF16), w_pool[0].astype(BF16), b_pool, pool_scale,
        w_branch_b[0].astype(BF16), w_out[0].astype(BF16))
    return _mlp(x, mod, norm_mlp_g, w_up[0].astype(BF16), w_down[0].astype(BF16),
                final_g[None, :])
```

```python
import jax
import jax.numpy as jnp
from jax import lax
from jax.experimental import pallas as pl
from jax.experimental.pallas import tpu as pltpu

D_MODEL = 1024
N_MOD = 6
N_BLOCKS = 4
BLOCK = D_MODEL // N_BLOCKS
CONV_WIDTH = 4
C_RG = 8.0
POOL_WINDOWS = (2, 4, 8, 16)
D_FF = 4 * D_MODEL
EPS = 1e-6
LOG2_E = 1.4426950408889634

SUBLANES = 8
LANES = 128
MXU_WIDTH = 256
N_SLABS = D_MODEL // LANES
SLABS_PER_BLOCK = BLOCK // LANES
CONV_HIST = SUBLANES
POOL_PAD = SUBLANES
POOL_HIST = 2 * SUBLANES
POOL_BASE = POOL_PAD + POOL_HIST
N_CHUNKS = SUBLANES
CHUNK_PITCH_PAD = 4
T_MIX = 512
T_MLP = 1024
FF_CHUNK = 1024
VMEM_LIMIT = 56 * 1024 * 1024

BF16 = jnp.bfloat16
F32 = jnp.float32


def _dot(a, b):
    return jnp.dot(a, b, preferred_element_type=F32)


def _rms_normalize(x):
    return x * lax.rsqrt(jnp.mean(x * x, axis=-1, keepdims=True) + EPS)


def _sigmoid(x):
    return 0.5 * jnp.tanh(0.5 * x) + 0.5


def _panels(w):
    k, n = w.shape
    return w.astype(BF16).reshape(k, n // MXU_WIDTH, MXU_WIDTH).transpose(1, 0, 2)


def _panel_dot(a, w_ref, first_col, n_cols, rows=slice(None)):
    first = first_col // MXU_WIDTH
    return jnp.concatenate(
        [_dot(a, w_ref[first + n, rows, :]) for n in range(n_cols // MXU_WIDTH)], axis=1)


def _slab_rows(ref, c, start, size):
    return ref[pl.ds(c, 1, stride=2), pl.ds(start, size), :][0]


def _store_slab_rows(ref, c, start, val):
    ref[pl.ds(c, 1, stride=2), pl.ds(start, val.shape[0]), :] = val[None]


def _load_rows(ref, start, size):
    return jnp.concatenate(
        [_slab_rows(ref, c, start, size) for c in range(ref.shape[0])], axis=1)


def _store_rows(ref, start, val):
    for c in range(ref.shape[0]):
        ref[c, start:start + val.shape[0], :] = val[:, c * LANES:(c + 1) * LANES]


def _ada_kernel(c_ref, w_ref, b_ref, o_ref):
    c = c_ref[...]
    c_act = c * jax.nn.sigmoid(c)
    o_ref[...] = _dot(c_act.astype(BF16), w_ref[...].astype(BF16)) + b_ref[...]


def _ada(c, w_ada, b_ada):
    batch = c.shape[0]
    n_out = w_ada.shape[1]
    return pl.pallas_call(
        _ada_kernel,
        grid=(n_out // D_MODEL,),
        in_specs=[
            pl.BlockSpec((batch, D_MODEL), lambda n: (0, 0)),
            pl.BlockSpec((D_MODEL, D_MODEL), lambda n: (0, n)),
            pl.BlockSpec((1, D_MODEL), lambda n: (0, n)),
        ],
        out_specs=pl.BlockSpec((batch, D_MODEL), lambda n: (0, n)),
        out_shape=jax.ShapeDtypeStruct((batch, n_out), F32),
        compiler_params=pltpu.CompilerParams(dimension_semantics=("arbitrary",)),
        name="ada_mod",
    )(c, w_ada, b_ada)


def _linear_scan(a, b, carry_ref, a_s, b_s, h_s, p_s):
    t = a.shape[0]
    ch = t // N_CHUNKS
    pitch = ch + CHUNK_PITCH_PAD
    for c in range(N_SLABS):
        for r in range(N_CHUNKS):
            rows, lanes = slice(r * ch, (r + 1) * ch), slice(c * LANES, (c + 1) * LANES)
            _store_slab_rows(a_s, c, r * pitch, a[rows, lanes])
            _store_slab_rows(b_s, c, r * pitch, b[rows, lanes])

    def step(g):
        return pl.ds(g, N_CHUNKS, stride=pitch)

    h = b_s[:, step(0), :]
    p = a_s[:, step(0), :]
    h_s[:, step(0), :] = h
    p_s[:, step(0), :] = p
    for g in range(1, ch):
        a_g = a_s[:, step(g), :]
        h = a_g * h + b_s[:, step(g), :]
        p = a_g * p
        h_s[:, step(g), :] = h
        p_s[:, step(g), :] = p

    carry = carry_ref[:, 0:1, :]
    carries = []
    for r in range(N_CHUNKS):
        carries.append(carry)
        carry = p[:, r:r + 1, :] * carry + h[:, r:r + 1, :]
    carry_ref[:, 0:1, :] = carry

    cols = []
    for c in range(N_SLABS):
        pieces = [_slab_rows(h_s, c, r * pitch, ch) + _slab_rows(p_s, c, r * pitch, ch) * carries[r][c]
                  for r in range(N_CHUNKS)]
        cols.append(jnp.concatenate(pieces, axis=0))
    return jnp.concatenate(cols, axis=1)


def _mixer_kernel(x_ref, mod_ref, g_ref, w_in_ref, conv_w_ref, conv_b_ref, w_rg_ref,
                  b_rga_ref, b_rgx_ref, a_param_ref, w_ba_ref, w_pool_ref, b_pool_ref,
                  pscale_ref, w_bb_ref, w_out_ref, o_ref,
                  x_s, u_s, s2_s, s4_s, carry_ref, a_s, b_s, h_s, p_s):
    j = pl.program_id(1)
    t = x_ref.shape[1]
    d = D_MODEL

    @pl.when(j == 0)
    def _():
        x_s[:, 0:CONV_HIST, :] = jnp.zeros((N_SLABS, CONV_HIST, LANES), F32)
        u_s[:, 0:POOL_BASE, :] = jnp.zeros((u_s.shape[0], POOL_BASE, LANES), F32)
        s2_s[:, 0:POOL_PAD, :] = jnp.zeros((s2_s.shape[0], POOL_PAD, LANES), F32)
        s4_s[:, 0:POOL_PAD, :] = jnp.zeros((s4_s.shape[0], POOL_PAD, LANES), F32)
        carry_ref[...] = jnp.zeros_like(carry_ref)

    x = x_ref[0]
    sh1 = mod_ref[0, 0:1, :]
    sc1 = mod_ref[0, 1:2, :]
    gt1 = mod_ref[0, 2:3, :]
    h = _rms_normalize(x) * (g_ref[...] * (1.0 + sc1)) + sh1
    hb = h.astype(BF16)

    x_rnn = _panel_dot(hb, w_in_ref, 0, d)
    _store_rows(x_s, CONV_HIST, x_rnn)
    xr = conv_b_ref[...] + x_rnn * conv_w_ref[CONV_WIDTH - 1:CONV_WIDTH, :]
    for k in range(1, CONV_WIDTH):
        xr = xr + _load_rows(x_s, CONV_HIST - k, t) * conv_w_ref[CONV_WIDTH - 1 - k:CONV_WIDTH - k, :]
    _store_rows(x_s, 0, x_rnn[t - CONV_HIST:])

    xrb = xr.astype(BF16)
    gates = [_dot(xrb[:, k * BLOCK:(k + 1) * BLOCK], w_rg_ref[k]) for k in range(N_BLOCKS)]
    r_pre = jnp.concatenate([g[:, :BLOCK] for g in gates], axis=1)
    i_pre = jnp.concatenate([g[:, BLOCK:] for g in gates], axis=1)
    r = _sigmoid(r_pre + b_rga_ref[...])
    i = _sigmoid(i_pre + b_rgx_ref[...])
    a = jnp.exp2(r * (-C_RG * LOG2_E * jax.nn.softplus(a_param_ref[...])))
    z = 1.0 - a * a
    mult = jnp.where(z > 0.0, z * lax.rsqrt(z), 0.0)
    gated_x = xr * i
    bval = gated_x * mult
    is_start = (lax.broadcasted_iota(jnp.int32, (SUBLANES, d), 0) == 0) & (j == 0)
    bval = jnp.concatenate(
        [jnp.where(is_start, gated_x[:SUBLANES], bval[:SUBLANES]), bval[SUBLANES:]], axis=0)
    hr = _linear_scan(a, bval, carry_ref, a_s, b_s, h_s, p_s)
    y_rnn = _panel_dot(hb, w_in_ref, d, d)
    branch_a = _panel_dot((jax.nn.gelu(y_rnn) * hr).astype(BF16), w_ba_ref, 0, d)

    u = _panel_dot(hb, w_in_ref, 2 * d, d)
    _store_rows(u_s, POOL_BASE, u)
    ext = POOL_HIST + t
    s2 = _load_rows(u_s, POOL_PAD, ext) + _load_rows(u_s, POOL_PAD - 1, ext)
    _store_rows(s2_s, POOL_PAD, s2[:, BLOCK:])
    s4 = s2[:, BLOCK:] + _load_rows(s2_s, POOL_PAD - 2, ext)
    _store_rows(s4_s, POOL_PAD, s4[:, BLOCK:])
    s8 = s4[:, BLOCK:] + _load_rows(s4_s, POOL_PAD - 4, ext)
    s16 = s8[POOL_HIST:, BLOCK:] + s8[POOL_HIST - SUBLANES:ext - SUBLANES, BLOCK:]
    _store_rows(u_s, POOL_PAD, u[t - POOL_HIST:])
    sums = (s2[POOL_HIST:, :BLOCK], s4[POOL_HIST:, :BLOCK], s8[POOL_HIST:, :BLOCK], s16)
    head_pos = (lax.broadcasted_iota(jnp.int32, (POOL_HIST, BLOCK), 0) + j * t + 1).astype(F32)
    pooled = []
    for gi, win in enumerate(POOL_WINDOWS):
        head = sums[gi][:POOL_HIST] / jnp.minimum(head_pos, float(win))
        mean = jnp.concatenate([head, sums[gi][POOL_HIST:] * (1.0 / win)], axis=0)
        p = mean - u[:, gi * BLOCK:(gi + 1) * BLOCK]
        pooled.append(_dot(p.astype(BF16), w_pool_ref[gi]))
    pooled = (jnp.concatenate(pooled, axis=1) + b_pool_ref[...]) * pscale_ref[...]
    branch_b = _panel_dot(pooled.astype(BF16), w_bb_ref, 0, d)

    g_a = _panel_dot(hb, w_in_ref, 3 * d, d)
    g_b = _panel_dot(hb, w_in_ref, 4 * d, d)
    merged = _sigmoid(g_a) * branch_a + _sigmoid(g_b) * branch_b
    o_ref[0] = x + gt1 * _panel_dot(merged.astype(BF16), w_out_ref, 0, d)


def _const_spec(shape):
    zeros = (0,) * len(shape)
    return pl.BlockSpec(shape, lambda b, j: zeros, pipeline_mode=pl.Buffered(1))


def _mixer(x, mod, g, w_in, conv_w, conv_b, w_rg, b_rga, b_rgx, a_param, w_ba, w_pool,
           b_pool, pscale, w_bb, w_out):
    batch, seq, d = x.shape
    weights = (g, w_in, conv_w, conv_b, w_rg, b_rga, b_rgx, a_param, w_ba, w_pool, b_pool,
               pscale, w_bb, w_out)
    scan_rows = N_CHUNKS * (T_MIX // N_CHUNKS + CHUNK_PITCH_PAD)
    scan_scratch = pltpu.VMEM((N_SLABS, scan_rows, LANES), F32)
    return pl.pallas_call(
        _mixer_kernel,
        grid=(batch, seq // T_MIX),
        in_specs=[
            pl.BlockSpec((1, T_MIX, d), lambda b, j: (b, j, 0)),
            pl.BlockSpec((1, N_MOD, d), lambda b, j: (b, 0, 0)),
        ] + [_const_spec(w.shape) for w in weights],
        out_specs=pl.BlockSpec((1, T_MIX, d), lambda b, j: (b, j, 0)),
        out_shape=jax.ShapeDtypeStruct(x.shape, F32),
        scratch_shapes=[
            pltpu.VMEM((N_SLABS, CONV_HIST + T_MIX, LANES), F32),
            pltpu.VMEM((N_SLABS, POOL_BASE + T_MIX, LANES), F32),
            pltpu.VMEM((N_SLABS - SLABS_PER_BLOCK, POOL_BASE + T_MIX, LANES), F32),
            pltpu.VMEM((N_SLABS - 2 * SLABS_PER_BLOCK, POOL_BASE + T_MIX, LANES), F32),
            pltpu.VMEM((N_SLABS, SUBLANES, LANES), F32),
            scan_scratch, scan_scratch, scan_scratch, scan_scratch,
        ],
        compiler_params=pltpu.CompilerParams(
            dimension_semantics=("arbitrary", "arbitrary"), vmem_limit_bytes=VMEM_LIMIT),
        name="mixer",
    )(x, mod, *weights)


def _mlp_kernel(x_ref, mod_ref, g_ref, w_up_ref, w_down_ref, gf_ref, o_ref):
    x = x_ref[0]
    sh2 = mod_ref[0, 3:4, :]
    sc2 = mod_ref[0, 4:5, :]
    gt2 = mod_ref[0, 5:6, :]
    h = _rms_normalize(x) * (g_ref[...] * (1.0 + sc2)) + sh2
    hb = h.astype(BF16)
    acc = jnp.zeros(x.shape, F32)
    for n in range(D_FF // FF_CHUNK):
        up = _panel_dot(hb, w_up_ref, n * FF_CHUNK, FF_CHUNK)
        ff = jnp.square(jnp.maximum(up, 0.0)).astype(BF16)
        acc = acc + _panel_dot(ff, w_down_ref, 0, D_MODEL, slice(n * FF_CHUNK, (n + 1) * FF_CHUNK))
    x2 = x + gt2 * acc
    o_ref[0] = _rms_normalize(x2) * gf_ref[...]


def _mlp(x, mod, g, w_up, w_down, final_g):
    batch, seq, d = x.shape
    weights = (g, w_up, w_down, final_g)
    return pl.pallas_call(
        _mlp_kernel,
        grid=(batch, seq // T_MLP),
        in_specs=[
            pl.BlockSpec((1, T_MLP, d), lambda b, j: (b, j, 0)),
            pl.BlockSpec((1, N_MOD, d), lambda b, j: (b, 0, 0)),
        ] + [_const_spec(w.shape) for w in weights],
        out_specs=pl.BlockSpec((1, T_MLP, d), lambda b, j: (b, j, 0)),
        out_shape=jax.ShapeDtypeStruct(x.shape, F32),
        compiler_params=pltpu.CompilerParams(
            dimension_semantics=("arbitrary", "arbitrary"), vmem_limit_bytes=VMEM_LIMIT),
        name="mlp_final",
    )(x, mod, *weights)


def kernel(x, c, norm_mix_g, norm_mlp_g, w_ada, b_ada, w_in, conv_w, conv_b, w_rg_a, b_rg_a,
           w_rg_x, b_rg_x, a_param, w_branch_a, w_pool, b_pool, pool_scale, w_branch_b, w_out,
           w_up, w_down, final_g):
    assert w_in.shape[0] == 1, "only DEPTH == 1 is supported"
    batch = x.shape[0]
    mod = _ada(c, w_ada[0], b_ada).reshape(batch, N_MOD, D_MODEL)
    w_rg = jnp.concatenate([w_rg_a[0], w_rg_x[0]], axis=-1).astype(BF16)
    x = _mixer(
        x, mod, norm_mix_g, _panels(w_in[0]), conv_w[0], conv_b, w_rg, b_rg_a, b_rg_x,
        a_param, _panels(w_branch_a[0]), w_pool[0].astype(BF16), b_pool, pool_scale,
        _panels(w_branch_b[0]), _panels(w_out[0]))
    return _mlp(x, mod, norm_mlp_g, _panels(w_up[0]), _panels(w_down[0]), final_g[None, :])
```

```python
import jax
import jax.numpy as jnp
from jax import lax
from jax.experimental import pallas as pl
from jax.experimental.pallas import tpu as pltpu

D_MODEL = 1024
N_MOD = 6
N_BLOCKS = 4
BLOCK = D_MODEL // N_BLOCKS
CONV_WIDTH = 4
C_RG = 8.0
POOL_WINDOWS = (2, 4, 8, 16)
D_FF = 4 * D_MODEL
EPS = 1e-6
LOG2_E = 1.4426950408889634

SUBLANES = 8
LANES = 128
MXU_WIDTH = 256
N_SLABS = D_MODEL // LANES
SLABS_PER_BLOCK = BLOCK // LANES
CONV_HIST = SUBLANES
POOL_PAD = SUBLANES
POOL_HIST = 2 * SUBLANES
POOL_BASE = POOL_PAD + POOL_HIST
N_CHUNKS = SUBLANES
CHUNK_PITCH_PAD = 4
T_MIX = 512
T_MLP = 1024
FF_CHUNK = 1024
VMEM_LIMIT = 56 * 1024 * 1024

BF16 = jnp.bfloat16
F32 = jnp.float32


def _dot(a, b):
    return jnp.dot(a, b, preferred_element_type=F32)


def _rms_normalize(x):
    return x * lax.rsqrt(jnp.mean(x * x, axis=-1, keepdims=True) + EPS)


def _sigmoid(x):
    return 0.5 * jnp.tanh(0.5 * x) + 0.5


class _Panels:
    def __init__(self, w):
        self.w = w
        self.count = w.shape[1] // MXU_WIDTH

    def operands(self):
        return [self.w] * self.count

    def specs(self):
        k = self.w.shape[0]
        return [pl.BlockSpec((k, MXU_WIDTH), (lambda *_, p=p: (0, p)), pipeline_mode=pl.Buffered(1))
                for p in range(self.count)]


def _panel_dot(a, panels, first_col, n_cols, rows=slice(None)):
    first = first_col // MXU_WIDTH
    return jnp.concatenate(
        [_dot(a, panels[first + n][rows, :]) for n in range(n_cols // MXU_WIDTH)], axis=1)


def _operands_and_specs(weights):
    operands, specs, sizes = [], [], []
    for w in weights:
        if isinstance(w, _Panels):
            operands += w.operands()
            specs += w.specs()
            sizes.append(w.count)
        else:
            operands.append(w)
            specs.append(_const_spec(w.shape))
            sizes.append(None)
    return operands, specs, sizes


def _regroup(kernel, n_leading, sizes):
    def wrapped(*refs):
        args, i = list(refs[:n_leading]), n_leading
        for size in sizes:
            if size is None:
                args.append(refs[i])
                i += 1
            else:
                args.append(tuple(refs[i:i + size]))
                i += size
        return kernel(*args, *refs[i:])
    return wrapped


def _slab_rows(ref, c, start, size):
    return ref[pl.ds(c, 1, stride=2), pl.ds(start, size), :][0]


def _store_slab_rows(ref, c, start, val):
    ref[pl.ds(c, 1, stride=2), pl.ds(start, val.shape[0]), :] = val[None]


def _load_rows(ref, start, size):
    return jnp.concatenate(
        [_slab_rows(ref, c, start, size) for c in range(ref.shape[0])], axis=1)


def _store_rows(ref, start, val):
    for c in range(ref.shape[0]):
        ref[c, start:start + val.shape[0], :] = val[:, c * LANES:(c + 1) * LANES]


def _ada_kernel(c_ref, w_ref, b_ref, o_ref):
    c = c_ref[...]
    c_act = c * jax.nn.sigmoid(c)
    o_ref[...] = _dot(c_act.astype(BF16), w_ref[...].astype(BF16)) + b_ref[...]


def _ada(c, w_ada, b_ada):
    batch = c.shape[0]
    n_out = w_ada.shape[1]
    return pl.pallas_call(
        _ada_kernel,
        grid=(n_out // D_MODEL,),
        in_specs=[
            pl.BlockSpec((batch, D_MODEL), lambda n: (0, 0)),
            pl.BlockSpec((D_MODEL, D_MODEL), lambda n: (0, n)),
            pl.BlockSpec((1, D_MODEL), lambda n: (0, n)),
        ],
        out_specs=pl.BlockSpec((batch, D_MODEL), lambda n: (0, n)),
        out_shape=jax.ShapeDtypeStruct((batch, n_out), F32),
        compiler_params=pltpu.CompilerParams(dimension_semantics=("arbitrary",)),
        name="ada_mod",
    )(c, w_ada, b_ada)


def _linear_scan(a, b, carry_ref, a_s, b_s, h_s, p_s):
    t = a.shape[0]
    ch = t // N_CHUNKS
    pitch = ch + CHUNK_PITCH_PAD
    for c in range(N_SLABS):
        for r in range(N_CHUNKS):
            rows, lanes = slice(r * ch, (r + 1) * ch), slice(c * LANES, (c + 1) * LANES)
            _store_slab_rows(a_s, c, r * pitch, a[rows, lanes])
            _store_slab_rows(b_s, c, r * pitch, b[rows, lanes])

    def step(g):
        return pl.ds(g, N_CHUNKS, stride=pitch)

    h = b_s[:, step(0), :]
    p = a_s[:, step(0), :]
    h_s[:, step(0), :] = h
    p_s[:, step(0), :] = p
    for g in range(1, ch):
        a_g = a_s[:, step(g), :]
        h = a_g * h + b_s[:, step(g), :]
        p = a_g * p
        h_s[:, step(g), :] = h
        p_s[:, step(g), :] = p

    carry = carry_ref[:, 0:1, :]
    carries = []
    for r in range(N_CHUNKS):
        carries.append(carry)
        carry = p[:, r:r + 1, :] * carry + h[:, r:r + 1, :]
    carry_ref[:, 0:1, :] = carry

    cols = []
    for c in range(N_SLABS):
        pieces = [_slab_rows(h_s, c, r * pitch, ch) + _slab_rows(p_s, c, r * pitch, ch) * carries[r][c]
                  for r in range(N_CHUNKS)]
        cols.append(jnp.concatenate(pieces, axis=0))
    return jnp.concatenate(cols, axis=1)


def _mixer_kernel(x_ref, mod_ref, g_ref, w_in_ref, conv_w_ref, conv_b_ref, w_rg_ref,
                  b_rga_ref, b_rgx_ref, a_param_ref, w_ba_ref, w_pool_ref, b_pool_ref,
                  pscale_ref, w_bb_ref, w_out_ref, o_ref,
                  x_s, u_s, s2_s, s4_s, carry_ref, a_s, b_s, h_s, p_s):
    j = pl.program_id(1)
    t = x_ref.shape[1]
    d = D_MODEL

    @pl.when(j == 0)
    def _():
        x_s[:, 0:CONV_HIST, :] = jnp.zeros((N_SLABS, CONV_HIST, LANES), F32)
        u_s[:, 0:POOL_BASE, :] = jnp.zeros((u_s.shape[0], POOL_BASE, LANES), F32)
        s2_s[:, 0:POOL_PAD, :] = jnp.zeros((s2_s.shape[0], POOL_PAD, LANES), F32)
        s4_s[:, 0:POOL_PAD, :] = jnp.zeros((s4_s.shape[0], POOL_PAD, LANES), F32)
        carry_ref[...] = jnp.zeros_like(carry_ref)

    x = x_ref[0]
    sh1 = mod_ref[0, 0:1, :]
    sc1 = mod_ref[0, 1:2, :]
    gt1 = mod_ref[0, 2:3, :]
    h = _rms_normalize(x) * (g_ref[...] * (1.0 + sc1)) + sh1
    hb = h.astype(BF16)

    x_rnn = _panel_dot(hb, w_in_ref, 0, d)
    _store_rows(x_s, CONV_HIST, x_rnn)
    xr = conv_b_ref[...] + x_rnn * conv_w_ref[CONV_WIDTH - 1:CONV_WIDTH, :]
    for k in range(1, CONV_WIDTH):
        xr = xr + _load_rows(x_s, CONV_HIST - k, t) * conv_w_ref[CONV_WIDTH - 1 - k:CONV_WIDTH - k, :]
    _store_rows(x_s, 0, x_rnn[t - CONV_HIST:])

    xrb = xr.astype(BF16)
    gates = [_dot(xrb[:, k * BLOCK:(k + 1) * BLOCK], w_rg_ref[k]) for k in range(N_BLOCKS)]
    r_pre = jnp.concatenate([g[:, :BLOCK] for g in gates], axis=1)
    i_pre = jnp.concatenate([g[:, BLOCK:] for g in gates], axis=1)
    r = _sigmoid(r_pre + b_rga_ref[...])
    i = _sigmoid(i_pre + b_rgx_ref[...])
    a = jnp.exp2(r * (-C_RG * LOG2_E * jax.nn.softplus(a_param_ref[...])))
    z = 1.0 - a * a
    mult = jnp.where(z > 0.0, z * lax.rsqrt(z), 0.0)
    gated_x = xr * i
    bval = gated_x * mult
    is_start = (lax.broadcasted_iota(jnp.int32, (SUBLANES, d), 0) == 0) & (j == 0)
    bval = jnp.concatenate(
        [jnp.where(is_start, gated_x[:SUBLANES], bval[:SUBLANES]), bval[SUBLANES:]], axis=0)
    hr = _linear_scan(a, bval, carry_ref, a_s, b_s, h_s, p_s)
    y_rnn = _panel_dot(hb, w_in_ref, d, d)
    branch_a = _panel_dot((jax.nn.gelu(y_rnn) * hr).astype(BF16), w_ba_ref, 0, d)

    u = _panel_dot(hb, w_in_ref, 2 * d, d)
    _store_rows(u_s, POOL_BASE, u)
    ext = POOL_HIST + t
    s2 = _load_rows(u_s, POOL_PAD, ext) + _load_rows(u_s, POOL_PAD - 1, ext)
    _store_rows(s2_s, POOL_PAD, s2[:, BLOCK:])
    s4 = s2[:, BLOCK:] + _load_rows(s2_s, POOL_PAD - 2, ext)
    _store_rows(s4_s, POOL_PAD, s4[:, BLOCK:])
    s8 = s4[:, BLOCK:] + _load_rows(s4_s, POOL_PAD - 4, ext)
    s16 = s8[POOL_HIST:, BLOCK:] + s8[POOL_HIST - SUBLANES:ext - SUBLANES, BLOCK:]
    _store_rows(u_s, POOL_PAD, u[t - POOL_HIST:])
    sums = (s2[POOL_HIST:, :BLOCK], s4[POOL_HIST:, :BLOCK], s8[POOL_HIST:, :BLOCK], s16)
    head_pos = (lax.broadcasted_iota(jnp.int32, (POOL_HIST, BLOCK), 0) + j * t + 1).astype(F32)
    pooled = []
    for gi, win in enumerate(POOL_WINDOWS):
        head = sums[gi][:POOL_HIST] / jnp.minimum(head_pos, float(win))
        mean = jnp.concatenate([head, sums[gi][POOL_HIST:] * (1.0 / win)], axis=0)
        p = mean - u[:, gi * BLOCK:(gi + 1) * BLOCK]
        pooled.append(_dot(p.astype(BF16), w_pool_ref[gi]))
    pooled = (jnp.concatenate(pooled, axis=1) + b_pool_ref[...]) * pscale_ref[...]
    branch_b = _panel_dot(pooled.astype(BF16), w_bb_ref, 0, d)

    g_a = _panel_dot(hb, w_in_ref, 3 * d, d)
    g_b = _panel_dot(hb, w_in_ref, 4 * d, d)
    merged = _sigmoid(g_a) * branch_a + _sigmoid(g_b) * branch_b
    o_ref[0] = x + gt1 * _panel_dot(merged.astype(BF16), w_out_ref, 0, d)


def _const_spec(shape):
    zeros = (0,) * len(shape)
    return pl.BlockSpec(shape, lambda b, j: zeros, pipeline_mode=pl.Buffered(1))


def _mixer(x, mod, g, w_in, conv_w, conv_b, w_rg, b_rga, b_rgx, a_param, w_ba, w_pool,
           b_pool, pscale, w_bb, w_out):
    batch, seq, d = x.shape
    weights, weight_specs, sizes = _operands_and_specs(
        (g, _Panels(w_in), conv_w, conv_b, w_rg, b_rga, b_rgx, a_param, _Panels(w_ba), w_pool,
         b_pool, pscale, _Panels(w_bb), _Panels(w_out)))
    scan_rows = N_CHUNKS * (T_MIX // N_CHUNKS + CHUNK_PITCH_PAD)
    scan_scratch = pltpu.VMEM((N_SLABS, scan_rows, LANES), F32)
    return pl.pallas_call(
        _regroup(_mixer_kernel, 2, sizes),
        grid=(batch, seq // T_MIX),
        in_specs=[
            pl.BlockSpec((1, T_MIX, d), lambda b, j: (b, j, 0)),
            pl.BlockSpec((1, N_MOD, d), lambda b, j: (b, 0, 0)),
        ] + weight_specs,
        out_specs=pl.BlockSpec((1, T_MIX, d), lambda b, j: (b, j, 0)),
        out_shape=jax.ShapeDtypeStruct(x.shape, F32),
        scratch_shapes=[
            pltpu.VMEM((N_SLABS, CONV_HIST + T_MIX, LANES), F32),
            pltpu.VMEM((N_SLABS, POOL_BASE + T_MIX, LANES), F32),
            pltpu.VMEM((N_SLABS - SLABS_PER_BLOCK, POOL_BASE + T_MIX, LANES), F32),
            pltpu.VMEM((N_SLABS - 2 * SLABS_PER_BLOCK, POOL_BASE + T_MIX, LANES), F32),
            pltpu.VMEM((N_SLABS, SUBLANES, LANES), F32),
            scan_scratch, scan_scratch, scan_scratch, scan_scratch,
        ],
        compiler_params=pltpu.CompilerParams(
            dimension_semantics=("arbitrary", "arbitrary"), vmem_limit_bytes=VMEM_LIMIT),
        name="mixer",
    )(x, mod, *weights)


def _mlp_kernel(x_ref, mod_ref, g_ref, w_up_ref, w_down_ref, gf_ref, o_ref):
    x = x_ref[0]
    sh2 = mod_ref[0, 3:4, :]
    sc2 = mod_ref[0, 4:5, :]
    gt2 = mod_ref[0, 5:6, :]
    h = _rms_normalize(x) * (g_ref[...] * (1.0 + sc2)) + sh2
    hb = h.astype(BF16)
    acc = jnp.zeros(x.shape, F32)
    for n in range(D_FF // FF_CHUNK):
        up = _panel_dot(hb, w_up_ref, n * FF_CHUNK, FF_CHUNK)
        ff = jnp.square(jnp.maximum(up, 0.0)).astype(BF16)
        acc = acc + _panel_dot(ff, w_down_ref, 0, D_MODEL, slice(n * FF_CHUNK, (n + 1) * FF_CHUNK))
    x2 = x + gt2 * acc
    o_ref[0] = _rms_normalize(x2) * gf_ref[...]


def _mlp(x, mod, g, w_up, w_down, final_g):
    batch, seq, d = x.shape
    weights, weight_specs, sizes = _operands_and_specs(
        (g, _Panels(w_up), _Panels(w_down), final_g))
    return pl.pallas_call(
        _regroup(_mlp_kernel, 2, sizes),
        grid=(batch, seq // T_MLP),
        in_specs=[
            pl.BlockSpec((1, T_MLP, d), lambda b, j: (b, j, 0)),
            pl.BlockSpec((1, N_MOD, d), lambda b, j: (b, 0, 0)),
        ] + weight_specs,
        out_specs=pl.BlockSpec((1, T_MLP, d), lambda b, j: (b, j, 0)),
        out_shape=jax.ShapeDtypeStruct(x.shape, F32),
        compiler_params=pltpu.CompilerParams(
            dimension_semantics=("arbitrary", "arbitrary"), vmem_limit_bytes=VMEM_LIMIT),
        name="mlp_final",
    )(x, mod, *weights)


def kernel(x, c, norm_mix_g, norm_mlp_g, w_ada, b_ada, w_in, conv_w, conv_b, w_rg_a, b_rg_a,
           w_rg_x, b_rg_x, a_param, w_branch_a, w_pool, b_pool, pool_scale, w_branch_b, w_out,
           w_up, w_down, final_g):
    assert w_in.shape[0] == 1, "only DEPTH == 1 is supported"
    batch = x.shape[0]
    mod = _ada(c, w_ada[0], b_ada).reshape(batch, N_MOD, D_MODEL)
    w_rg = jnp.concatenate([w_rg_a[0], w_rg_x[0]], axis=-1).astype(BF16)
    x = _mixer(
        x, mod, norm_mix_g, w_in[0].astype(BF16), conv_w[0], conv_b, w_rg, b_rg_a, b_rg_x,
        a_param, w_branch_a[0].astype(BF16), w_pool[0].astype(BF16), b_pool, pool_scale,
        w_branch_b[0].astype(BF16), w_out[0].astype(BF16))
    return _mlp(x, mod, norm_mlp_g, w_up[0].astype(BF16), w_down[0].astype(BF16),
                final_g[None, :])
```

```python
import jax
import jax.numpy as jnp
from jax import lax
from jax.experimental import pallas as pl
from jax.experimental.pallas import tpu as pltpu

D_MODEL = 1024
N_MOD = 6
N_BLOCKS = 4
BLOCK = D_MODEL // N_BLOCKS
CONV_WIDTH = 4
C_RG = 8.0
POOL_WINDOWS = (2, 4, 8, 16)
D_FF = 4 * D_MODEL
EPS = 1e-6
LOG2_E = 1.4426950408889634

SUBLANES = 8
LANES = 128
MXU_WIDTH = 256
N_SLABS = D_MODEL // LANES
SLABS_PER_BLOCK = BLOCK // LANES
CONV_HIST = SUBLANES
POOL_PAD = SUBLANES
POOL_HIST = 2 * SUBLANES
POOL_BASE = POOL_PAD + POOL_HIST
N_CHUNKS = SUBLANES
CHUNK_PITCH_PAD = 4
T_MIX = 512
T_MLP = 1024
FF_CHUNK = 1024
VMEM_LIMIT = 56 * 1024 * 1024

BF16 = jnp.bfloat16
F32 = jnp.float32


def _dot(a, b):
    return jnp.dot(a, b, preferred_element_type=F32)


def _rms_normalize(x):
    return x * lax.rsqrt(jnp.mean(x * x, axis=-1, keepdims=True) + EPS)


def _sigmoid(x):
    return 0.5 * jnp.tanh(0.5 * x) + 0.5


class _Panels:
    def __init__(self, w):
        self.w = w
        self.count = w.shape[1] // MXU_WIDTH

    def operands(self):
        return [self.w] * self.count

    def specs(self):
        k = self.w.shape[0]
        return [pl.BlockSpec((k, MXU_WIDTH), (lambda *_, p=p: (0, p)), pipeline_mode=pl.Buffered(1))
                for p in range(self.count)]


def _panel_dot(a, panels, first_col, n_cols, rows=slice(None)):
    first = first_col // MXU_WIDTH
    return jnp.concatenate(
        [_dot(a, panels[first + n][rows, :]) for n in range(n_cols // MXU_WIDTH)], axis=1)


def _operands_and_specs(weights):
    operands, specs, sizes = [], [], []
    for w in weights:
        if isinstance(w, _Panels):
            operands += w.operands()
            specs += w.specs()
            sizes.append(w.count)
        else:
            operands.append(w)
            specs.append(_const_spec(w.shape))
            sizes.append(None)
    return operands, specs, sizes


def _regroup(kernel, n_leading, sizes):
    def wrapped(*refs):
        args, i = list(refs[:n_leading]), n_leading
        for size in sizes:
            if size is None:
                args.append(refs[i])
                i += 1
            else:
                args.append(tuple(refs[i:i + size]))
                i += size
        return kernel(*args, *refs[i:])
    return wrapped


def _slab_rows(ref, c, start, size):
    return ref[pl.ds(c, 1, stride=2), pl.ds(start, size), :][0]


def _store_slab_rows(ref, c, start, val):
    ref[pl.ds(c, 1, stride=2), pl.ds(start, val.shape[0]), :] = val[None]


def _load_rows(ref, start, size):
    return jnp.concatenate(
        [_slab_rows(ref, c, start, size) for c in range(ref.shape[0])], axis=1)


def _store_rows(ref, start, val):
    for c in range(ref.shape[0]):
        ref[c, start:start + val.shape[0], :] = val[:, c * LANES:(c + 1) * LANES]


def _ada_kernel(c_ref, w_ref, b_ref, o_ref):
    c = c_ref[...]
    c_act = c * jax.nn.sigmoid(c)
    o_ref[...] = _dot(c_act.astype(BF16), w_ref[...].astype(BF16)) + b_ref[...]


def _ada(c, w_ada, b_ada):
    batch = c.shape[0]
    n_out = w_ada.shape[1]
    return pl.pallas_call(
        _ada_kernel,
        grid=(n_out // D_MODEL,),
        in_specs=[
            pl.BlockSpec((batch, D_MODEL), lambda n: (0, 0)),
            pl.BlockSpec((D_MODEL, D_MODEL), lambda n: (0, n)),
            pl.BlockSpec((1, D_MODEL), lambda n: (0, n)),
        ],
        out_specs=pl.BlockSpec((batch, D_MODEL), lambda n: (0, n)),
        out_shape=jax.ShapeDtypeStruct((batch, n_out), F32),
        compiler_params=pltpu.CompilerParams(dimension_semantics=("arbitrary",)),
        name="ada_mod",
    )(c, w_ada, b_ada)


def _linear_scan(a, b, carry_ref, a_s, b_s, h_s, p_s):
    t = a.shape[0]
    ch = t // N_CHUNKS
    pitch = ch + CHUNK_PITCH_PAD
    for c in range(N_SLABS):
        for r in range(N_CHUNKS):
            rows, lanes = slice(r * ch, (r + 1) * ch), slice(c * LANES, (c + 1) * LANES)
            _store_slab_rows(a_s, c, r * pitch, a[rows, lanes])
            _store_slab_rows(b_s, c, r * pitch, b[rows, lanes])

    def step(g):
        return pl.ds(g, N_CHUNKS, stride=pitch)

    h = b_s[:, step(0), :]
    p = a_s[:, step(0), :]
    h_s[:, step(0), :] = h
    p_s[:, step(0), :] = p
    for g in range(1, ch):
        a_g = a_s[:, step(g), :]
        h = a_g * h + b_s[:, step(g), :]
        p = a_g * p
        h_s[:, step(g), :] = h
        p_s[:, step(g), :] = p

    carry = carry_ref[:, 0:1, :]
    carries = []
    for r in range(N_CHUNKS):
        carries.append(carry)
        carry = p[:, r:r + 1, :] * carry + h[:, r:r + 1, :]
    carry_ref[:, 0:1, :] = carry

    cols = []
    for c in range(N_SLABS):
        pieces = [_slab_rows(h_s, c, r * pitch, ch) + _slab_rows(p_s, c, r * pitch, ch) * carries[r][c]
                  for r in range(N_CHUNKS)]
        cols.append(jnp.concatenate(pieces, axis=0))
    return jnp.concatenate(cols, axis=1)


def _mixer_kernel(x_ref, mod_ref, g_ref, w_in_ref, conv_w_ref, conv_b_ref, w_rg_ref,
                  b_rga_ref, b_rgx_ref, a_param_ref, w_ba_ref, w_pool_ref, b_pool_ref,
                  pscale_ref, w_bb_ref, w_out_ref, o_ref,
                  x_s, u_s, s2_s, s4_s, carry_ref, a_s, b_s, h_s, p_s):
    j = pl.program_id(1)
    t = x_ref.shape[1]
    d = D_MODEL

    @pl.when(j == 0)
    def _():
        x_s[:, 0:CONV_HIST, :] = jnp.zeros((N_SLABS, CONV_HIST, LANES), F32)
        u_s[:, 0:POOL_BASE, :] = jnp.zeros((u_s.shape[0], POOL_BASE, LANES), F32)
        s2_s[:, 0:POOL_PAD, :] = jnp.zeros((s2_s.shape[0], POOL_PAD, LANES), F32)
        s4_s[:, 0:POOL_PAD, :] = jnp.zeros((s4_s.shape[0], POOL_PAD, LANES), F32)
        carry_ref[...] = jnp.zeros_like(carry_ref)

    x = x_ref[0]
    sh1 = mod_ref[0, 0:1, :]
    sc1 = mod_ref[0, 1:2, :]
    gt1 = mod_ref[0, 2:3, :]
    h = _rms_normalize(x) * (g_ref[...] * (1.0 + sc1)) + sh1
    hb = h.astype(BF16)

    x_rnn = _panel_dot(hb, w_in_ref, 0, d)
    y_rnn = _panel_dot(hb, w_in_ref, d, d)
    _store_rows(x_s, CONV_HIST, x_rnn)
    xr = conv_b_ref[...] + x_rnn * conv_w_ref[CONV_WIDTH - 1:CONV_WIDTH, :]
    for k in range(1, CONV_WIDTH):
        xr = xr + _load_rows(x_s, CONV_HIST - k, t) * conv_w_ref[CONV_WIDTH - 1 - k:CONV_WIDTH - k, :]
    _store_rows(x_s, 0, x_rnn[t - CONV_HIST:])

    xrb = xr.astype(BF16)
    gates = [_dot(xrb[:, k * BLOCK:(k + 1) * BLOCK], w_rg_ref[k]) for k in range(N_BLOCKS)]
    r_pre = jnp.concatenate([g[:, :BLOCK] for g in gates], axis=1)
    i_pre = jnp.concatenate([g[:, BLOCK:] for g in gates], axis=1)
    u = _panel_dot(hb, w_in_ref, 2 * d, d)
    g_a = _panel_dot(hb, w_in_ref, 3 * d, d)
    t_r = jnp.tanh(r_pre + 0.5 * b_rga_ref[...])
    i = 0.5 * jnp.tanh(i_pre + 0.5 * b_rgx_ref[...]) + 0.5
    half_scale = (-0.5 * C_RG * LOG2_E) * jax.nn.softplus(a_param_ref[...])
    a = jnp.exp2(t_r * half_scale + half_scale)
    z = 1.0 - a * a
    mult = jnp.where(z > 0.0, z * lax.rsqrt(z), 0.0)
    gated_x = xr * i
    bval = gated_x * mult
    is_start = (lax.broadcasted_iota(jnp.int32, (SUBLANES, d), 0) == 0) & (j == 0)
    bval = jnp.concatenate(
        [jnp.where(is_start, gated_x[:SUBLANES], bval[:SUBLANES]), bval[SUBLANES:]], axis=0)
    g_b = _panel_dot(hb, w_in_ref, 4 * d, d)
    hr = _linear_scan(a, bval, carry_ref, a_s, b_s, h_s, p_s)
    branch_a = _panel_dot((jax.nn.gelu(y_rnn) * hr).astype(BF16), w_ba_ref, 0, d)

    _store_rows(u_s, POOL_BASE, u)
    ext = POOL_HIST + t
    s2 = _load_rows(u_s, POOL_PAD, ext) + _load_rows(u_s, POOL_PAD - 1, ext)
    _store_rows(s2_s, POOL_PAD, s2[:, BLOCK:])
    s4 = s2[:, BLOCK:] + _load_rows(s2_s, POOL_PAD - 2, ext)
    _store_rows(s4_s, POOL_PAD, s4[:, BLOCK:])
    s8 = s4[:, BLOCK:] + _load_rows(s4_s, POOL_PAD - 4, ext)
    s16 = s8[POOL_HIST:, BLOCK:] + s8[POOL_HIST - SUBLANES:ext - SUBLANES, BLOCK:]
    _store_rows(u_s, POOL_PAD, u[t - POOL_HIST:])
    sums = (s2[POOL_HIST:, :BLOCK], s4[POOL_HIST:, :BLOCK], s8[POOL_HIST:, :BLOCK], s16)
    head_pos = (lax.broadcasted_iota(jnp.int32, (POOL_HIST, BLOCK), 0) + j * t + 1).astype(F32)
    pooled = []
    for gi, win in enumerate(POOL_WINDOWS):
        head = sums[gi][:POOL_HIST] / jnp.minimum(head_pos, float(win))
        mean = jnp.concatenate([head, sums[gi][POOL_HIST:] * (1.0 / win)], axis=0)
        p = mean - u[:, gi * BLOCK:(gi + 1) * BLOCK]
        pooled.append(_dot(p.astype(BF16), w_pool_ref[gi]))
    pooled = (jnp.concatenate(pooled, axis=1) + b_pool_ref[...]) * pscale_ref[...]
    branch_b = _panel_dot(pooled.astype(BF16), w_bb_ref, 0, d)

    merged = _sigmoid(g_a) * branch_a + _sigmoid(g_b) * branch_b
    o_ref[0] = x + gt1 * _panel_dot(merged.astype(BF16), w_out_ref, 0, d)


def _const_spec(shape):
    zeros = (0,) * len(shape)
    return pl.BlockSpec(shape, lambda b, j: zeros, pipeline_mode=pl.Buffered(1))


def _mixer(x, mod, g, w_in, conv_w, conv_b, w_rg, b_rga, b_rgx, a_param, w_ba, w_pool,
           b_pool, pscale, w_bb, w_out):
    batch, seq, d = x.shape
    weights, weight_specs, sizes = _operands_and_specs(
        (g, _Panels(w_in), conv_w, conv_b, w_rg, b_rga, b_rgx, a_param, _Panels(w_ba), w_pool,
         b_pool, pscale, _Panels(w_bb), _Panels(w_out)))
    scan_rows = N_CHUNKS * (T_MIX // N_CHUNKS + CHUNK_PITCH_PAD)
    scan_scratch = pltpu.VMEM((N_SLABS, scan_rows, LANES), F32)
    return pl.pallas_call(
        _regroup(_mixer_kernel, 2, sizes),
        grid=(batch, seq // T_MIX),
        in_specs=[
            pl.BlockSpec((1, T_MIX, d), lambda b, j: (b, j, 0)),
            pl.BlockSpec((1, N_MOD, d), lambda b, j: (b, 0, 0)),
        ] + weight_specs,
        out_specs=pl.BlockSpec((1, T_MIX, d), lambda b, j: (b, j, 0)),
        out_shape=jax.ShapeDtypeStruct(x.shape, F32),
        scratch_shapes=[
            pltpu.VMEM((N_SLABS, CONV_HIST + T_MIX, LANES), F32),
            pltpu.VMEM((N_SLABS, POOL_BASE + T_MIX, LANES), F32),
            pltpu.VMEM((N_SLABS - SLABS_PER_BLOCK, POOL_BASE + T_MIX, LANES), F32),
            pltpu.VMEM((N_SLABS - 2 * SLABS_PER_BLOCK, POOL_BASE + T_MIX, LANES), F32),
            pltpu.VMEM((N_SLABS, SUBLANES, LANES), F32),
            scan_scratch, scan_scratch, scan_scratch, scan_scratch,
        ],
        compiler_params=pltpu.CompilerParams(
            dimension_semantics=("arbitrary", "arbitrary"), vmem_limit_bytes=VMEM_LIMIT),
        name="mixer",
    )(x, mod, *weights)


def _mlp_kernel(x_ref, mod_ref, g_ref, w_up_ref, w_down_ref, gf_ref, o_ref):
    x = x_ref[0]
    sh2 = mod_ref[0, 3:4, :]
    sc2 = mod_ref[0, 4:5, :]
    gt2 = mod_ref[0, 5:6, :]
    h = _rms_normalize(x) * (g_ref[...] * (1.0 + sc2)) + sh2
    hb = h.astype(BF16)
    acc = jnp.zeros(x.shape, F32)
    for n in range(D_FF // FF_CHUNK):
        up = _panel_dot(hb, w_up_ref, n * FF_CHUNK, FF_CHUNK)
        ff = jnp.square(jnp.maximum(up, 0.0)).astype(BF16)
        acc = acc + _panel_dot(ff, w_down_ref, 0, D_MODEL, slice(n * FF_CHUNK, (n + 1) * FF_CHUNK))
    x2 = x + gt2 * acc
    o_ref[0] = _rms_normalize(x2) * gf_ref[...]


def _mlp(x, mod, g, w_up, w_down, final_g):
    batch, seq, d = x.shape
    weights, weight_specs, sizes = _operands_and_specs(
        (g, _Panels(w_up), _Panels(w_down), final_g))
    return pl.pallas_call(
        _regroup(_mlp_kernel, 2, sizes),
        grid=(batch, seq // T_MLP),
        in_specs=[
            pl.BlockSpec((1, T_MLP, d), lambda b, j: (b, j, 0)),
            pl.BlockSpec((1, N_MOD, d), lambda b, j: (b, 0, 0)),
        ] + weight_specs,
        out_specs=pl.BlockSpec((1, T_MLP, d), lambda b, j: (b, j, 0)),
        out_shape=jax.ShapeDtypeStruct(x.shape, F32),
        compiler_params=pltpu.CompilerParams(
            dimension_semantics=("arbitrary", "arbitrary"), vmem_limit_bytes=VMEM_LIMIT),
        name="mlp_final",
    )(x, mod, *weights)


def kernel(x, c, norm_mix_g, norm_mlp_g, w_ada, b_ada, w_in, conv_w, conv_b, w_rg_a, b_rg_a,
           w_rg_x, b_rg_x, a_param, w_branch_a, w_pool, b_pool, pool_scale, w_branch_b, w_out,
           w_up, w_down, final_g):
    assert w_in.shape[0] == 1, "only DEPTH == 1 is supported"
    batch = x.shape[0]
    mod = _ada(c, w_ada[0], b_ada).reshape(batch, N_MOD, D_MODEL)
    w_rg = (0.5 * jnp.concatenate([w_rg_a[0], w_rg_x[0]], axis=-1)).astype(BF16)
    x = _mixer(
        x, mod, norm_mix_g, w_in[0].astype(BF16), conv_w[0], conv_b, w_rg, b_rg_a, b_rg_x,
        a_param, w_branch_a[0].astype(BF16), w_pool[0].astype(BF16), b_pool, pool_scale,
        w_branch_b[0].astype(BF16), w_out[0].astype(BF16))
    return _mlp(x, mod, norm_mlp_g, w_up[0].astype(BF16), w_down[0].astype(BF16),
                final_g[None, :])
```

```python
import functools

import jax
import jax.numpy as jnp
from jax import lax
from jax.experimental import pallas as pl
from jax.experimental.pallas import tpu as pltpu

D_MODEL = 1024
N_MOD = 6
N_BLOCKS = 4
BLOCK = D_MODEL // N_BLOCKS
CONV_WIDTH = 4
C_RG = 8.0
POOL_WINDOWS = (2, 4, 8, 16)
D_FF = 4 * D_MODEL
EPS = 1e-6
LOG2_E = 1.4426950408889634

SUBLANES = 8
LANES = 128
MXU_WIDTH = 256
N_SLABS = D_MODEL // LANES
SLABS_PER_BLOCK = BLOCK // LANES
CONV_HIST = SUBLANES
POOL_PAD = SUBLANES
POOL_HIST = 2 * SUBLANES
POOL_BASE = POOL_PAD + POOL_HIST
N_CHUNKS = SUBLANES
CHUNK_PITCH_PAD = 4
T_MIX = 512
T_MLP = 1024
FF_CHUNK = 1024
VMEM_LIMIT = 56 * 1024 * 1024

BF16 = jnp.bfloat16
F32 = jnp.float32


def _dot(a, b):
    return jnp.dot(a, b, preferred_element_type=F32)


def _rms_normalize(x):
    return x * lax.rsqrt(jnp.mean(x * x, axis=-1, keepdims=True) + EPS)


def _sigmoid(x):
    return 0.5 * jnp.tanh(0.5 * x) + 0.5


class _Panels:
    def __init__(self, w):
        self.w = w
        self.count = w.shape[1] // MXU_WIDTH

    def operands(self):
        return [self.w] * self.count

    def specs(self):
        k = self.w.shape[0]
        return [pl.BlockSpec((k, MXU_WIDTH), (lambda *_, p=p: (0, p)), pipeline_mode=pl.Buffered(1))
                for p in range(self.count)]


def _panel_dot(a, panels, first_col, n_cols, rows=slice(None)):
    first = first_col // MXU_WIDTH
    return jnp.concatenate(
        [_dot(a, panels[first + n][rows, :]) for n in range(n_cols // MXU_WIDTH)], axis=1)


def _operands_and_specs(weights):
    operands, specs, sizes = [], [], []
    for w in weights:
        if isinstance(w, _Panels):
            operands += w.operands()
            specs += w.specs()
            sizes.append(w.count)
        else:
            operands.append(w)
            specs.append(_const_spec(w.shape))
            sizes.append(None)
    return operands, specs, sizes


def _regroup(kernel, n_leading, sizes):
    def wrapped(*refs):
        args, i = list(refs[:n_leading]), n_leading
        for size in sizes:
            if size is None:
                args.append(refs[i])
                i += 1
            else:
                args.append(tuple(refs[i:i + size]))
                i += size
        return kernel(*args, *refs[i:])
    return wrapped


def _slab_rows(ref, c, start, size):
    return ref[pl.ds(c, 1, stride=2), pl.ds(start, size), :][0]


def _store_slab_rows(ref, c, start, val):
    ref[pl.ds(c, 1, stride=2), pl.ds(start, val.shape[0]), :] = val[None]


def _load_rows(ref, start, size):
    return jnp.concatenate(
        [_slab_rows(ref, c, start, size) for c in range(ref.shape[0])], axis=1)


def _store_rows(ref, start, val):
    for c in range(ref.shape[0]):
        ref[c, start:start + val.shape[0], :] = val[:, c * LANES:(c + 1) * LANES]


def _ada_kernel(c_ref, w_ref, b_ref, o_ref):
    c = c_ref[...]
    c_act = c * jax.nn.sigmoid(c)
    o_ref[...] = _dot(c_act.astype(BF16), w_ref[...].astype(BF16)) + b_ref[...]


def _ada(c, w_ada, b_ada):
    batch = c.shape[0]
    n_out = w_ada.shape[1]
    return pl.pallas_call(
        _ada_kernel,
        grid=(n_out // D_MODEL,),
        in_specs=[
            pl.BlockSpec((batch, D_MODEL), lambda n: (0, 0)),
            pl.BlockSpec((D_MODEL, D_MODEL), lambda n: (0, n)),
            pl.BlockSpec((1, D_MODEL), lambda n: (0, n)),
        ],
        out_specs=pl.BlockSpec((batch, D_MODEL), lambda n: (0, n)),
        out_shape=jax.ShapeDtypeStruct((batch, n_out), F32),
        compiler_params=pltpu.CompilerParams(dimension_semantics=("arbitrary",)),
        name="ada_mod",
    )(c, w_ada, b_ada)


def _linear_scan(a, b, carry_ref, a_s, b_s, h_s, p_s):
    t = a.shape[0]
    ch = t // N_CHUNKS
    pitch = ch + CHUNK_PITCH_PAD
    for c in range(N_SLABS):
        for r in range(N_CHUNKS):
            rows, lanes = slice(r * ch, (r + 1) * ch), slice(c * LANES, (c + 1) * LANES)
            _store_slab_rows(a_s, c, r * pitch, a[rows, lanes])
            _store_slab_rows(b_s, c, r * pitch, b[rows, lanes])

    def step(g):
        return pl.ds(g, N_CHUNKS, stride=pitch)

    h = b_s[:, step(0), :]
    p = a_s[:, step(0), :]
    h_s[:, step(0), :] = h
    p_s[:, step(0), :] = p
    for g in range(1, ch):
        a_g = a_s[:, step(g), :]
        h = a_g * h + b_s[:, step(g), :]
        p = a_g * p
        h_s[:, step(g), :] = h
        p_s[:, step(g), :] = p

    carry = carry_ref[:, 0:1, :]
    carries = []
    for r in range(N_CHUNKS):
        carries.append(carry)
        carry = p[:, r:r + 1, :] * carry + h[:, r:r + 1, :]
    carry_ref[:, 0:1, :] = carry

    cols = []
    for c in range(N_SLABS):
        pieces = [_slab_rows(h_s, c, r * pitch, ch) + _slab_rows(p_s, c, r * pitch, ch) * carries[r][c]
                  for r in range(N_CHUNKS)]
        cols.append(jnp.concatenate(pieces, axis=0))
    return jnp.concatenate(cols, axis=1)


def _mixer_kernel(blocks_per_seq, x_ref, mod_ref, wup_ref, wdown_ref,
                  g_ref, w_in_ref, conv_w_ref, conv_b_ref, w_rg_ref,
                  b_rga_ref, b_rgx_ref, a_param_ref, w_ba_ref, w_pool_ref, b_pool_ref,
                  pscale_ref, w_bb_ref, w_out_ref, o_ref, wup_bf_ref, wdown_bf_ref,
                  x_s, u_s, s2_s, s4_s, carry_ref, a_s, b_s, h_s, p_s):
    j = pl.program_id(0) % blocks_per_seq
    t = x_ref.shape[1]
    d = D_MODEL

    @pl.when(j == 0)
    def _():
        x_s[:, 0:CONV_HIST, :] = jnp.zeros((N_SLABS, CONV_HIST, LANES), F32)
        u_s[:, 0:POOL_BASE, :] = jnp.zeros((u_s.shape[0], POOL_BASE, LANES), F32)
        s2_s[:, 0:POOL_PAD, :] = jnp.zeros((s2_s.shape[0], POOL_PAD, LANES), F32)
        s4_s[:, 0:POOL_PAD, :] = jnp.zeros((s4_s.shape[0], POOL_PAD, LANES), F32)
        carry_ref[...] = jnp.zeros_like(carry_ref)

    wup_bf_ref[...] = wup_ref[...].astype(BF16)
    wdown_bf_ref[...] = wdown_ref[...].astype(BF16)

    x = x_ref[0]
    sh1 = mod_ref[0, 0:1, :]
    sc1 = mod_ref[0, 1:2, :]
    gt1 = mod_ref[0, 2:3, :]
    hb = (_rms_normalize(x) * (g_ref[...] * (1.0 + sc1)) + sh1).astype(BF16)

    x_rnn = _panel_dot(hb, w_in_ref, 0, d)
    y_rnn = _panel_dot(hb, w_in_ref, d, d)
    _store_rows(x_s, CONV_HIST, x_rnn)
    xr = conv_b_ref[...] + x_rnn * conv_w_ref[CONV_WIDTH - 1:CONV_WIDTH, :]
    for k in range(1, CONV_WIDTH):
        xr = xr + _load_rows(x_s, CONV_HIST - k, t) * conv_w_ref[CONV_WIDTH - 1 - k:CONV_WIDTH - k, :]
    _store_rows(x_s, 0, x_rnn[t - CONV_HIST:])

    xrb = xr.astype(BF16)
    gates = [_dot(xrb[:, k * BLOCK:(k + 1) * BLOCK], w_rg_ref[k]) for k in range(N_BLOCKS)]
    r_pre = jnp.concatenate([g[:, :BLOCK] for g in gates], axis=1)
    i_pre = jnp.concatenate([g[:, BLOCK:] for g in gates], axis=1)
    u = _panel_dot(hb, w_in_ref, 2 * d, d)
    g_a = _panel_dot(hb, w_in_ref, 3 * d, d)
    t_r = jnp.tanh(r_pre + 0.5 * b_rga_ref[...])
    i = 0.5 * jnp.tanh(i_pre + 0.5 * b_rgx_ref[...]) + 0.5
    half_scale = (-0.5 * C_RG * LOG2_E) * jax.nn.softplus(a_param_ref[...])
    a = jnp.exp2(t_r * half_scale + half_scale)
    z = 1.0 - a * a
    mult = jnp.where(z > 0.0, z * lax.rsqrt(z), 0.0)
    gated_x = xr * i
    bval = gated_x * mult
    is_start = (lax.broadcasted_iota(jnp.int32, (SUBLANES, d), 0) == 0) & (j == 0)
    bval = jnp.concatenate(
        [jnp.where(is_start, gated_x[:SUBLANES], bval[:SUBLANES]), bval[SUBLANES:]], axis=0)
    g_b = _panel_dot(hb, w_in_ref, 4 * d, d)
    hr = _linear_scan(a, bval, carry_ref, a_s, b_s, h_s, p_s)
    branch_a = _panel_dot((jax.nn.gelu(y_rnn) * hr).astype(BF16), w_ba_ref, 0, d)

    _store_rows(u_s, POOL_BASE, u)
    ext = POOL_HIST + t
    s2 = _load_rows(u_s, POOL_PAD, ext) + _load_rows(u_s, POOL_PAD - 1, ext)
    _store_rows(s2_s, POOL_PAD, s2[:, BLOCK:])
    s4 = s2[:, BLOCK:] + _load_rows(s2_s, POOL_PAD - 2, ext)
    _store_rows(s4_s, POOL_PAD, s4[:, BLOCK:])
    s8 = s4[:, BLOCK:] + _load_rows(s4_s, POOL_PAD - 4, ext)
    s16 = s8[POOL_HIST:, BLOCK:] + s8[POOL_HIST - SUBLANES:ext - SUBLANES, BLOCK:]
    _store_rows(u_s, POOL_PAD, u[t - POOL_HIST:])
    sums = (s2[POOL_HIST:, :BLOCK], s4[POOL_HIST:, :BLOCK], s8[POOL_HIST:, :BLOCK], s16)
    head_pos = (lax.broadcasted_iota(jnp.int32, (POOL_HIST, BLOCK), 0) + j * t + 1).astype(F32)
    pooled = []
    for gi, win in enumerate(POOL_WINDOWS):
        head = sums[gi][:POOL_HIST] / jnp.minimum(head_pos, float(win))
        mean = jnp.concatenate([head, sums[gi][POOL_HIST:] * (1.0 / win)], axis=0)
        p = mean - u[:, gi * BLOCK:(gi + 1) * BLOCK]
        pooled.append(_dot(p.astype(BF16), w_pool_ref[gi]))
    pooled = (jnp.concatenate(pooled, axis=1) + b_pool_ref[...]) * pscale_ref[...]
    branch_b = _panel_dot(pooled.astype(BF16), w_bb_ref, 0, d)

    merged = (_sigmoid(g_a) * branch_a + _sigmoid(g_b) * branch_b).astype(BF16)
    o_ref[0] = x + gt1 * _panel_dot(merged, w_out_ref, 0, d)


def _const_spec(shape):
    zeros = (0,) * len(shape)
    return pl.BlockSpec(shape, lambda *_: zeros, pipeline_mode=pl.Buffered(1))


def _mixer(x, mod, w_up, w_down, g, w_in, conv_w, conv_b, w_rg, b_rga, b_rgx, a_param, w_ba,
           w_pool, b_pool, pscale, w_bb, w_out):
    batch, seq, d = x.shape
    weights, weight_specs, sizes = _operands_and_specs(
        (g, _Panels(w_in), conv_w, conv_b, w_rg, b_rga, b_rgx, a_param, _Panels(w_ba), w_pool,
         b_pool, pscale, _Panels(w_bb), _Panels(w_out)))
    blocks_per_seq = seq // T_MIX
    n_steps = batch * blocks_per_seq
    up_rows, down_rows = w_up.shape[0] // n_steps, w_down.shape[0] // n_steps
    assert up_rows * n_steps == w_up.shape[0] and down_rows * n_steps == w_down.shape[0]

    def x_map(s):
        return s // blocks_per_seq, s % blocks_per_seq, 0

    scan_rows = N_CHUNKS * (T_MIX // N_CHUNKS + CHUNK_PITCH_PAD)
    scan_scratch = pltpu.VMEM((N_SLABS, scan_rows, LANES), F32)
    return pl.pallas_call(
        _regroup(functools.partial(_mixer_kernel, blocks_per_seq), 4, sizes),
        grid=(n_steps,),
        in_specs=[
            pl.BlockSpec((1, T_MIX, d), x_map),
            pl.BlockSpec((1, N_MOD, d), lambda s: (s // blocks_per_seq, 0, 0)),
            pl.BlockSpec((up_rows, w_up.shape[1]), lambda s: (s, 0)),
            pl.BlockSpec((down_rows, w_down.shape[1]), lambda s: (s, 0)),
        ] + weight_specs,
        out_specs=[
            pl.BlockSpec((1, T_MIX, d), x_map),
            pl.BlockSpec((up_rows, w_up.shape[1]), lambda s: (s, 0)),
            pl.BlockSpec((down_rows, w_down.shape[1]), lambda s: (s, 0)),
        ],
        out_shape=[
            jax.ShapeDtypeStruct(x.shape, F32),
            jax.ShapeDtypeStruct(w_up.shape, BF16),
            jax.ShapeDtypeStruct(w_down.shape, BF16),
        ],
        scratch_shapes=[
            pltpu.VMEM((N_SLABS, CONV_HIST + T_MIX, LANES), F32),
            pltpu.VMEM((N_SLABS, POOL_BASE + T_MIX, LANES), F32),
            pltpu.VMEM((N_SLABS - SLABS_PER_BLOCK, POOL_BASE + T_MIX, LANES), F32),
            pltpu.VMEM((N_SLABS - 2 * SLABS_PER_BLOCK, POOL_BASE + T_MIX, LANES), F32),
            pltpu.VMEM((N_SLABS, SUBLANES, LANES), F32),
            scan_scratch, scan_scratch, scan_scratch, scan_scratch,
        ],
        compiler_params=pltpu.CompilerParams(
            dimension_semantics=("arbitrary",), vmem_limit_bytes=VMEM_LIMIT),
        name="mixer",
    )(x, mod, w_up, w_down, *weights)


def _mlp_kernel(x_ref, mod_ref, g_ref, w_up_ref, w_down_ref, gf_ref, o_ref):
    x = x_ref[0]
    sh2 = mod_ref[0, 3:4, :]
    sc2 = mod_ref[0, 4:5, :]
    gt2 = mod_ref[0, 5:6, :]
    h = _rms_normalize(x) * (g_ref[...] * (1.0 + sc2)) + sh2
    hb = h.astype(BF16)
    acc = jnp.zeros(x.shape, F32)
    for n in range(D_FF // FF_CHUNK):
        up = _panel_dot(hb, w_up_ref, n * FF_CHUNK, FF_CHUNK)
        ff = jnp.square(jnp.maximum(up, 0.0)).astype(BF16)
        acc = acc + _panel_dot(ff, w_down_ref, 0, D_MODEL, slice(n * FF_CHUNK, (n + 1) * FF_CHUNK))
    x2 = x + gt2 * acc
    o_ref[0] = _rms_normalize(x2) * gf_ref[...]


def _mlp(x, mod, g, w_up, w_down, final_g):
    batch, seq, d = x.shape
    weights, weight_specs, sizes = _operands_and_specs(
        (g, _Panels(w_up), _Panels(w_down), final_g))
    return pl.pallas_call(
        _regroup(_mlp_kernel, 2, sizes),
        grid=(batch, seq // T_MLP),
        in_specs=[
            pl.BlockSpec((1, T_MLP, d), lambda b, j: (b, j, 0)),
            pl.BlockSpec((1, N_MOD, d), lambda b, j: (b, 0, 0)),
        ] + weight_specs,
        out_specs=pl.BlockSpec((1, T_MLP, d), lambda b, j: (b, j, 0)),
        out_shape=jax.ShapeDtypeStruct(x.shape, F32),
        compiler_params=pltpu.CompilerParams(
            dimension_semantics=("arbitrary", "arbitrary"), vmem_limit_bytes=VMEM_LIMIT),
        name="mlp_final",
    )(x, mod, *weights)


def kernel(x, c, norm_mix_g, norm_mlp_g, w_ada, b_ada, w_in, conv_w, conv_b, w_rg_a, b_rg_a,
           w_rg_x, b_rg_x, a_param, w_branch_a, w_pool, b_pool, pool_scale, w_branch_b, w_out,
           w_up, w_down, final_g):
    assert w_in.shape[0] == 1, "only DEPTH == 1 is supported"
    batch = x.shape[0]
    mod = _ada(c, w_ada[0], b_ada).reshape(batch, N_MOD, D_MODEL)
    w_rg = (0.5 * jnp.concatenate([w_rg_a[0], w_rg_x[0]], axis=-1)).astype(BF16)
    x, w_up_bf, w_down_bf = _mixer(
        x, mod, w_up[0], w_down[0], norm_mix_g, w_in[0].astype(BF16), conv_w[0], conv_b, w_rg,
        b_rg_a, b_rg_x, a_param, w_branch_a[0].astype(BF16), w_pool[0].astype(BF16), b_pool,
        pool_scale, w_branch_b[0].astype(BF16), w_out[0].astype(BF16))
    return _mlp(x, mod, norm_mlp_g, w_up_bf, w_down_bf, final_g[None, :])
```
